```python
import jax
import jax.numpy as jnp
from jax import lax
import numpy as np


D_MODEL = 1024
BATCH = 4
SEQ = 8192
DEPTH = 2
DEC_BATCH = 16
DEC_SEQ = 4096
PAST_LEN = 128

PLE_DIM = 256
H_A = 4
DH_A = D_MODEL // H_A
W_A = H_A * DH_A
CHUNK_A = 128
CONV_W = 3
EXPAND_B = 128
H_B = D_MODEL // EXPAND_B
DK_B = EXPAND_B
DV_B = D_MODEL // H_B
F_B = H_B * DK_B
W_B = H_B * DV_B
CHUNK_B = 64
N_EXPERTS = 256
TOP_K = 8
N_GROUPS = 8
TOPK_GROUPS = 4
D_EXPERT = 256
ROUTED_SCALE = 2.5
MOE_BLOCK = 128
ALPHA = (2.0 * DEPTH) ** 0.25
BETA = (8.0 * DEPTH) ** -0.25
EPS = 1e-5
NEG = -1e30
TINY = 1e-30
SPLIT_SIZES = (W_A, W_A, W_A, W_A, 2 * H_A, 2 * H_A, F_B, F_B, F_B, W_B, W_B, D_MODEL, D_MODEL)
N_IN = sum(SPLIT_SIZES)

kernel_name = 'hybrid_mlstm_hgrn2_moe_encoder'


def _split_points():
    return [int(v) for v in np.cumsum(SPLIT_SIZES)[:-1]]


def _layer_norm(x, g, b):
    xf = x.astype(jnp.float32)
    xc = xf - jnp.mean(xf, axis=-1, keepdims=True)
    var = jnp.mean(xc * xc, axis=-1, keepdims=True)
    return (xc * lax.rsqrt(var + EPS) * g + b).astype(x.dtype)


def _head_norm(h, g, n_heads, center):
    hf = h.astype(jnp.float32).reshape(h.shape[:-1] + (n_heads, -1))
    if center:
        hf = hf - jnp.mean(hf, axis=-1, keepdims=True)
    hf = hf * lax.rsqrt(jnp.mean(hf * hf, axis=-1, keepdims=True) + EPS)
    return hf.reshape(h.shape) * g


def _heads(t, n_heads):
    b, s, _ = t.shape
    return t.reshape(b, s, n_heads, -1).transpose(0, 2, 1, 3)


def _merge_heads(t):
    b, h, s, d = t.shape
    return t.transpose(0, 2, 1, 3).reshape(b, s, h * d)


def _flip(t):
    return jnp.flip(t, axis=2)


def _dw_conv(t, w, b):
    out = lax.conv_general_dilated(t, w[:, None, :].astype(t.dtype), window_strides=(1,),
                                   padding=[(CONV_W // 2, CONV_W // 2)],
                                   dimension_numbers=('NWC', 'WIO', 'NWC'),
                                   feature_group_count=t.shape[-1])
    return out + b


def _mlstm_dir(q, k, v, log_i, log_f):
    bsz, nh, slen, dk = q.shape
    dv = v.shape[-1]
    nc = slen // CHUNK_A
    tril = jnp.tril(jnp.ones((CHUNK_A, CHUNK_A), dtype=bool))

    def to_chunks(t):
        return jnp.moveaxis(t.reshape((bsz, nh, nc, CHUNK_A) + t.shape[3:]), 2, 0)

    def step(carry, xs):
        c_st, n_st, m_st = carry
        qc, kc, vc, ic, fc = xs
        bcum = jnp.cumsum(fc, axis=-1)
        d_mat = jnp.where(tril, bcum[..., :, None] - bcum[..., None, :] + ic[..., None, :], NEG)
        m_inter = bcum + m_st[..., None]
        m_row = jnp.maximum(m_inter, jnp.max(d_mat, axis=-1))
        w_inter = jnp.exp(m_inter - m_row)
        s_mat = jnp.einsum('bhjd,bhsd->bhjs', qc, kc) * jnp.exp(d_mat - m_row[..., None])
        num = (w_inter[..., None] * jnp.einsum('bhjd,bhde->bhje', qc, c_st)
               + jnp.einsum('bhjs,bhse->bhje', s_mat, vc))
        den = w_inter * jnp.einsum('bhjd,bhd->bhj', qc, n_st) + jnp.sum(s_mat, axis=-1)
        h = num / jnp.maximum(jnp.abs(den), jnp.exp(-m_row))[..., None]
        b_last = bcum[..., -1]
        g_end = b_last[..., None] - bcum + ic
        m_next = jnp.maximum(b_last + m_st, jnp.max(g_end, axis=-1))
        decay = jnp.exp(b_last + m_st - m_next)
        wk = jnp.exp(g_end - m_next[..., None])[..., None] * kc
        c_next = decay[..., None, None] * c_st + jnp.einsum('bhsd,bhse->bhde', wk, vc)
        n_next = decay[..., None] * n_st + jnp.sum(wk, axis=2)
        return (c_next, n_next, m_next), h

    init = (jnp.zeros((bsz, nh, dk, dv), jnp.float32), jnp.zeros((bsz, nh, dk), jnp.float32),
            jnp.zeros((bsz, nh), jnp.float32))
    _, h = lax.scan(step, init, tuple(to_chunks(t) for t in (q, k, v, log_i, log_f)))
    return jnp.moveaxis(h, 0, 2).reshape(bsz, nh, slen, dv)


def _hgrn2_dir(q, k, v, log_f):
    bsz, nh, slen, dk = q.shape
    dv = v.shape[-1]
    nc = slen // CHUNK_B
    tril = jnp.tril(jnp.ones((CHUNK_B, CHUNK_B), dtype=bool))

    def to_chunks(t):
        return jnp.moveaxis(t.reshape((bsz, nh, nc, CHUNK_B) + t.shape[3:]), 2, 0)

    def step(s_st, xs):
        qc, kc, vc, gc = xs
        bcum = jnp.cumsum(gc, axis=2)
        rel = jnp.where(tril[:, :, None], bcum[:, :, :, None, :] - bcum[:, :, None, :, :], NEG)
        a_mat = jnp.einsum('bhjsd,bhsd->bhjs', qc[:, :, :, None, :] * jnp.exp(rel), kc)
        o = (jnp.einsum('bhjd,bhde->bhje', qc * jnp.exp(bcum), s_st)
             + jnp.einsum('bhjs,bhse->bhje', a_mat, vc))
        b_last = bcum[:, :, -1]
        s_next = (jnp.exp(b_last)[..., None] * s_st
                  + jnp.einsum('bhsd,bhse->bhde', kc * jnp.exp(b_last[:, :, None, :] - bcum), vc))
        return s_next, o

    init = jnp.zeros((bsz, nh, dk, dv), jnp.float32)
    _, o = lax.scan(step, init, tuple(to_chunks(t) for t in (q, k, v, log_f)))
    return jnp.moveaxis(o, 0, 2).reshape(bsz, nh, slen, dv)


def _token_mixer(x, lower, w_in, b_in, conv_w, conv_b, norm_a, norm_b, w_out_a, w_out_b, w_o):
    f32 = jnp.float32
    proj = x @ w_in + b_in
    (qa, ka, va, oa, ia, fa, qb, ffb, fbb, ib, gb, gate_a, gate_b) = jnp.split(proj, _split_points(), axis=-1)
    qk = jax.nn.silu(_dw_conv(jnp.concatenate([qa, ka], axis=-1), conv_w, conv_b))
    qa, ka = jnp.split(qk, 2, axis=-1)
    q = _heads(qa, H_A).astype(f32)
    k = _heads(ka, H_A).astype(f32) * (DH_A ** -0.5)
    v = _heads(va, H_A).astype(f32)
    log_i = jnp.moveaxis(ia.astype(f32), -1, 1)
    log_f = jnp.moveaxis(jax.nn.log_sigmoid(fa.astype(f32)), -1, 1)
    h_fwd = _mlstm_dir(q, k, v, log_i[:, :H_A], log_f[:, :H_A])
    h_bwd = _flip(_mlstm_dir(_flip(q), _flip(k), _flip(v), _flip(log_i[:, H_A:]), _flip(log_f[:, H_A:])))
    h_a = jax.nn.sigmoid(oa) * _head_norm(_merge_heads(h_fwd + h_bwd), norm_a, H_A, True).astype(x.dtype)
    y_a = h_a @ w_out_a
    qh = _heads(jax.nn.silu(qb), H_B).astype(f32)
    vh = _heads(ib, H_B).astype(f32)

    def forget(z, lb):
        zf = z.astype(f32)
        k_in = (1.0 - lb) * jax.nn.sigmoid(-zf)
        lf = jnp.logaddexp(jnp.log(jnp.maximum(lb, TINY)), jnp.log1p(-lb) + jax.nn.log_sigmoid(zf))
        return _heads(k_in, H_B), _heads(lf, H_B)

    k_f, lf_f = forget(ffb, lower[0])
    k_b, lf_b = forget(fbb, lower[1])
    o_b = _hgrn2_dir(qh, k_f, vh, lf_f) + _flip(_hgrn2_dir(_flip(qh), _flip(k_b), _flip(vh), _flip(lf_b)))
    o_b = _head_norm(_merge_heads(o_b), norm_b, H_B, False).astype(x.dtype) * jax.nn.silu(gb)
    y_b = o_b @ w_out_b
    merged = jax.nn.sigmoid(gate_a) * y_a + jax.nn.sigmoid(gate_b) * y_b
    return merged @ w_o


def _routed_experts(xf, idx, wts, w1, w3, w2):
    n_tok, d = xf.shape
    n_asg = n_tok * TOP_K
    n_pad = -(-n_asg // MOE_BLOCK) * MOE_BLOCK + N_EXPERTS * MOE_BLOCK
    n_blk = n_pad // MOE_BLOCK
    flat_e = idx.reshape(-1)
    order = jnp.argsort(flat_e)
    e_sorted = flat_e[order]
    tok_sorted = (order // TOP_K).astype(jnp.int32)
    w_sorted = wts.reshape(-1)[order].astype(xf.dtype)
    counts = jnp.bincount(flat_e, length=N_EXPERTS)
    padded = (counts + MOE_BLOCK - 1) // MOE_BLOCK * MOE_BLOCK
    p_end = jnp.cumsum(padded)
    p_start = p_end - padded
    start = jnp.cumsum(counts) - counts
    dest = p_start[e_sorted] + jnp.arange(n_asg) - start[e_sorted]
    buf_tok = jnp.full((n_pad,), n_tok, jnp.int32).at[dest].set(tok_sorted)
    buf_w = jnp.zeros((n_pad,), xf.dtype).at[dest].set(w_sorted)
    blk_e = jnp.minimum(jnp.searchsorted(p_end, jnp.arange(n_blk) * MOE_BLOCK, side='right'), N_EXPERTS - 1)
    x_pad = jnp.concatenate([xf, jnp.zeros((1, d), xf.dtype)], axis=0)

    def step(acc, blk):
        t, w, e = blk
        xb = x_pad[t]
        hb = jax.nn.silu(xb @ w1[e]) * (xb @ w3[e])
        return acc.at[t].add((hb @ w2[e]) * w[:, None]), None

    acc, _ = lax.scan(step, jnp.zeros((n_tok + 1, d), xf.dtype),
                      (buf_tok.reshape(n_blk, MOE_BLOCK), buf_w.reshape(n_blk, MOE_BLOCK), blk_e))
    return acc[:n_tok]


def _moe(x, w_router, router_bias, w1, w3, w2, ws1, ws3, ws2):
    bsz, slen, d = x.shape
    xf = x.reshape(-1, d)
    n_tok = xf.shape[0]
    scores = jax.nn.sigmoid((xf @ w_router).astype(jnp.float32))
    sel = scores + router_bias.astype(jnp.float32)
    grp = sel.reshape(n_tok, N_GROUPS, N_EXPERTS // N_GROUPS)
    g_score = jnp.sum(lax.top_k(grp, 2)[0], axis=-1)
    _, g_idx = lax.top_k(g_score, TOPK_GROUPS)
    g_mask = jnp.sum(jax.nn.one_hot(g_idx, N_GROUPS, dtype=jnp.float32), axis=-2) > 0
    sel = jnp.where(jnp.repeat(g_mask, N_EXPERTS // N_GROUPS, axis=-1), sel, NEG)
    _, idx = lax.top_k(sel, TOP_K)
    wts = jnp.take_along_axis(scores, idx, axis=-1)
    wts = wts / jnp.sum(wts, axis=-1, keepdims=True) * ROUTED_SCALE
    routed = _routed_experts(xf, idx, wts, w1, w3, w2)
    shared = (jax.nn.silu(xf @ ws1) * (xf @ ws3)) @ ws2
    return (routed + shared).reshape(bsz, slen, d)


def setup_inputs(seed: int = 0) -> dict:
    key = jax.random.key(seed)
    ks = jax.random.split(key, 32)
    f32 = jnp.float32

    def nrm(k, shape, scale):
        return jax.random.normal(k, shape, f32) * scale

    lin = jnp.linspace(3.0, 6.0, H_A, dtype=f32)
    fg0 = 4 * W_A + 2 * H_A
    b_in = nrm(ks[7], (DEPTH, N_IN), 0.02).at[:, fg0:fg0 + 2 * H_A].add(jnp.concatenate([lin, lin]))
    return {
        'x_prompt': nrm(ks[0], (BATCH, SEQ, D_MODEL), 1.0),
        'x_sample': nrm(ks[1], (DEC_BATCH, DEC_SEQ, D_MODEL), 1.0),
        'p_prompt': nrm(ks[2], (DEPTH, BATCH, SEQ, PLE_DIM), 1.0),
        'p_sample': nrm(ks[3], (DEPTH, DEC_BATCH, DEC_SEQ, PLE_DIM), 1.0),
        'emb_ln_g': 1.0 + nrm(ks[4], (D_MODEL,), 0.02),
        'emb_ln_b': nrm(ks[5], (D_MODEL,), 0.02),
        'w_in': nrm(ks[6], (DEPTH, D_MODEL, N_IN), D_MODEL ** -0.5),
        'b_in': b_in,
        'conv_w': nrm(ks[8], (DEPTH, CONV_W, 2 * W_A), CONV_W ** -0.5),
        'conv_b': nrm(ks[9], (DEPTH, 2 * W_A), 0.02),
        'norm_a': 1.0 + nrm(ks[10], (DEPTH, W_A), 0.02),
        'norm_b': 1.0 + nrm(ks[11], (DEPTH, W_B), 0.02),
        'lb_param': 1.0 + nrm(ks[12], (DEPTH, 2, F_B), 0.1),
        'w_out_a': nrm(ks[13], (DEPTH, W_A, D_MODEL), BETA * W_A ** -0.5),
        'w_out_b': nrm(ks[14], (DEPTH, W_B, D_MODEL), BETA * W_B ** -0.5),
        'w_o': nrm(ks[15], (DEPTH, D_MODEL, D_MODEL), BETA * D_MODEL ** -0.5),
        'ln1_g': 1.0 + nrm(ks[16], (DEPTH, D_MODEL), 0.02),
        'ln1_b': nrm(ks[17], (DEPTH, D_MODEL), 0.02),
        'w_ple': nrm(ks[18], (DEPTH, PLE_DIM, D_MODEL), PLE_DIM ** -0.5),
        'w_ple_gate': nrm(ks[19], (DEPTH, D_MODEL, D_MODEL), D_MODEL ** -0.5),
        'w_router': nrm(ks[20], (DEPTH, D_MODEL, N_EXPERTS), D_MODEL ** -0.5),
        'router_bias': nrm(ks[21], (DEPTH, N_EXPERTS), 0.01),
        'w1': nrm(ks[22], (DEPTH, N_EXPERTS, D_MODEL, D_EXPERT), D_MODEL ** -0.5),
        'w3': nrm(ks[23], (DEPTH, N_EXPERTS, D_MODEL, D_EXPERT), D_MODEL ** -0.5),
        'w2': nrm(ks[24], (DEPTH, N_EXPERTS, D_EXPERT, D_MODEL), BETA * D_EXPERT ** -0.5),
        'ws1': nrm(ks[25], (DEPTH, D_MODEL, D_EXPERT), D_MODEL ** -0.5),
        'ws3': nrm(ks[26], (DEPTH, D_MODEL, D_EXPERT), D_MODEL ** -0.5),
        'ws2': nrm(ks[27], (DEPTH, D_EXPERT, D_MODEL), BETA * D_EXPERT ** -0.5),
        'ln2_g': 1.0 + nrm(ks[28], (DEPTH, D_MODEL), 0.02),
        'ln2_b': nrm(ks[29], (DEPTH, D_MODEL), 0.02),
    }


def reference(x_prompt, x_sample, p_prompt, p_sample, emb_ln_g, emb_ln_b, w_in, b_in, conv_w, conv_b,
              norm_a, norm_b, lb_param, w_out_a, w_out_b, w_o, ln1_g, ln1_b, w_ple, w_ple_gate,
              w_router, router_bias, w1, w3, w2, ws1, ws3, ws2, ln2_g, ln2_b):
    lb_soft = jax.nn.softmax(lb_param.astype(jnp.float32), axis=0)
    lower = jnp.cumsum(lb_soft, axis=0) - lb_soft[0:1]

    def trunk(x, p):
        x = _layer_norm(x, emb_ln_g, emb_ln_b)
        for li in range(DEPTH):
            mix = _token_mixer(x, lower[li], w_in[li], b_in[li], conv_w[li], conv_b[li], norm_a[li],
                               norm_b[li], w_out_a[li], w_out_b[li], w_o[li])
            x = _layer_norm(ALPHA * x + mix, ln1_g[li], ln1_b[li])
            ple = (p[li] @ w_ple[li]) * jax.nn.sigmoid(x @ w_ple_gate[li])
            ffn = _moe(x, w_router[li], router_bias[li], w1[li], w3[li], w2[li], ws1[li], ws3[li], ws2[li])
            x = _layer_norm(ALPHA * x + ffn + ple, ln2_g[li], ln2_b[li])
        return x

    y_prompt = trunk(x_prompt, p_prompt)
    y_sample = trunk(x_sample, p_sample)
    return (y_prompt, y_sample)
```

```python
import functools

import jax
import jax.numpy as jnp
from jax import lax
from jax.experimental import pallas as pl
from jax.experimental.pallas import tpu as pltpu

F32 = jnp.float32
BF16 = jnp.bfloat16

D_MODEL = 1024
PLE_DIM = 256
H_A = 4
DH_A = D_MODEL // H_A
CHUNK_A = 128
H_B = 8
DK_B = 128
CHUNK_B = 64
N_EXPERTS = 256
TOP_K = 8
N_GROUPS = 8
GROUP_SIZE = N_EXPERTS // N_GROUPS
TOPK_GROUPS = 4
D_EXPERT = 256
ROUTED_SCALE = 2.5
EPS = 1e-5
NEG = -1e30
TINY = 1e-30

HGRN_SAFE_RANGE = 80.0
MOE_BLOCK = 256
COMBINE_TILE = 128
VMEM_LIMIT = 56 * 1024 * 1024


def _cparams(*sem):
    return pltpu.CompilerParams(dimension_semantics=sem, vmem_limit_bytes=VMEM_LIMIT)


def _split3(x):
    hi = x.astype(BF16)
    r1 = x - hi.astype(F32)
    mid = r1.astype(BF16)
    lo = (r1 - mid.astype(F32)).astype(BF16)
    return hi, mid, lo


def _dot(a, b):
    return jnp.dot(a, b, preferred_element_type=F32)


def _dot_nt(a, b):
    return lax.dot_general(a, b, (((1,), (1,)), ((), ())), preferred_element_type=F32)


def _dot_tn(a, b):
    return lax.dot_general(a, b, (((0,), (0,)), ((), ())), preferred_element_type=F32)


def _layer_norm_rows(v, g, b):
    mu = jnp.mean(v, axis=-1, keepdims=True)
    vc = v - mu
    var = jnp.mean(vc * vc, axis=-1, keepdims=True)
    return vc * lax.rsqrt(var + EPS) * g + b


def _log_sigmoid(z):
    return jnp.minimum(z, 0.0) - jnp.log1p(jnp.exp(-jnp.abs(z)))


def _silu(z):
    return z * jax.nn.sigmoid(z)


def _ln_kernel(x_ref, g_ref, b_ref, o_ref, ob_ref):
    y = _layer_norm_rows(x_ref[...], g_ref[...], b_ref[...])
    o_ref[...] = y
    ob_ref[...] = y.astype(BF16)


def _layer_norm_call(x, g, b):
    t, d = x.shape
    tm = min(512, t)
    return pl.pallas_call(
        _ln_kernel,
        grid=(t // tm,),
        in_specs=[pl.BlockSpec((tm, d), lambda i: (i, 0)),
                  pl.BlockSpec((1, d), lambda i: (0, 0)),
                  pl.BlockSpec((1, d), lambda i: (0, 0))],
        out_specs=[pl.BlockSpec((tm, d), lambda i: (i, 0)),
                   pl.BlockSpec((tm, d), lambda i: (i, 0))],
        out_shape=[jax.ShapeDtypeStruct((t, d), F32), jax.ShapeDtypeStruct((t, d), BF16)],
        compiler_params=_cparams("parallel"),
        name="emb_layer_norm",
    )(x, g.reshape(1, d), b.reshape(1, d))


def _proj_act_kernel(x_ref, w_ref, b_ref, o_ref, *, act):
    z = _dot(x_ref[...], w_ref[...]) + b_ref[...]
    if act == "sigmoid":
        z = jax.nn.sigmoid(z)
    elif act == "silu":
        z = _silu(z)
    o_ref[...] = z.astype(o_ref.dtype)


def _proj_act(xb, w, b, act, name):
    t, d = xb.shape
    n = w.shape[1]
    tm = min(1024, t)
    tn = 1024
    return pl.pallas_call(
        functools.partial(_proj_act_kernel, act=act),
        grid=(t // tm, n // tn),
        in_specs=[pl.BlockSpec((tm, d), lambda i, j: (i, 0)),
                  pl.BlockSpec((d, tn), lambda i, j: (0, j)),
                  pl.BlockSpec((1, tn), lambda i, j: (0, j))],
        out_specs=pl.BlockSpec((tm, tn), lambda i, j: (i, j)),
        out_shape=jax.ShapeDtypeStruct((t, n), BF16),
        compiler_params=_cparams("parallel", "arbitrary"),
        name=name,
    )(xb, w, b.reshape(1, n))


def _proj_forget_kernel(x_ref, w_ref, b_ref, oml_ref, la_ref, lc_ref, k_ref, lf_ref):
    z = _dot(x_ref[...], w_ref[...]) + b_ref[...]
    k_ref[...] = (oml_ref[...] * jax.nn.sigmoid(-z)).astype(BF16)
    u = lc_ref[...] + _log_sigmoid(z)
    a = la_ref[...]
    lf_ref[...] = jnp.maximum(a, u) + jnp.log1p(jnp.exp(-jnp.abs(a - u)))


def _proj_forget(xb, w, b, lower):
    t, d = xb.shape
    n = w.shape[1]
    tm = min(1024, t)
    tn = 1024
    lb = lower.reshape(1, n)
    vec = lambda: pl.BlockSpec((1, tn), lambda i, j: (0, j))
    return pl.pallas_call(
        _proj_forget_kernel,
        grid=(t // tm, n // tn),
        in_specs=[pl.BlockSpec((tm, d), lambda i, j: (i, 0)),
                  pl.BlockSpec((d, tn), lambda i, j: (0, j)),
                  vec(), vec(), vec(), vec()],
        out_specs=[pl.BlockSpec((tm, tn), lambda i, j: (i, j)),
                   pl.BlockSpec((tm, tn), lambda i, j: (i, j))],
        out_shape=[jax.ShapeDtypeStruct((t, n), BF16), jax.ShapeDtypeStruct((t, n), F32)],
        compiler_params=_cparams("parallel", "arbitrary"),
        name="proj_forget",
    )(xb, w, b.reshape(1, n), 1.0 - lb, jnp.log(jnp.maximum(lb, TINY)), jnp.log1p(-lb))


def _proj_gates_kernel(x_ref, w_ref, wt_ref, b_ref, bt_ref, gc_ref, gr_ref):
    x = x_ref[...]
    zc = _dot(x, w_ref[...]) + b_ref[...]
    lane = lax.broadcasted_iota(jnp.int32, zc.shape, 1)
    gc_ref[...] = jnp.where(lane >= 2 * H_A, _log_sigmoid(zc), zc)
    zr = _dot_nt(wt_ref[...], x) + bt_ref[...]
    sub = lax.broadcasted_iota(jnp.int32, zr.shape, 0)
    gr_ref[...] = jnp.where(sub >= 2 * H_A, _log_sigmoid(zr), zr)


def _proj_gates(xb, w, b):
    t, d = xb.shape
    n = w.shape[1]
    tm = min(1024, t)
    return pl.pallas_call(
        _proj_gates_kernel,
        grid=(t // tm,),
        in_specs=[pl.BlockSpec((tm, d), lambda i: (i, 0)),
                  pl.BlockSpec((d, n), lambda i: (0, 0)),
                  pl.BlockSpec((n, d), lambda i: (0, 0)),
                  pl.BlockSpec((1, n), lambda i: (0, 0)),
                  pl.BlockSpec((n, 1), lambda i: (0, 0))],
        out_specs=[pl.BlockSpec((tm, n), lambda i: (i, 0)),
                   pl.BlockSpec((n, tm), lambda i: (0, i))],
        out_shape=[jax.ShapeDtypeStruct((t, n), F32), jax.ShapeDtypeStruct((n, t), F32)],
        compiler_params=_cparams("parallel"),
        name="proj_gates",
    )(xb, w, w.T, b.reshape(1, n), b.reshape(n, 1))


def _conv_kernel(x_ref, w_ref, b_ref, o_ref, *, rows):
    s = x_ref.shape[1]
    n_rc = s // rows
    w = w_ref[...]
    w0, w1, w2 = w[0:1, :], w[1:2, :], w[2:3, :]
    bias = b_ref[...]
    ridx = lax.broadcasted_iota(jnp.int32, (rows, x_ref.shape[2]), 0)

    def body(r, carry):
        base = pl.multiple_of(r * rows, rows)
        cur = x_ref[0, pl.ds(base, rows), :].astype(F32)
        pstart = pl.multiple_of(jnp.maximum(base - 16, 0), 16)
        nstart = pl.multiple_of(jnp.minimum(base + rows, s - 16), 16)
        prev_row = x_ref[0, pl.ds(pstart, 16), :][15:16, :].astype(F32)
        next_row = x_ref[0, pl.ds(nstart, 16), :][0:1, :].astype(F32)
        prev_row = jnp.where(r > 0, prev_row, 0.0)
        next_row = jnp.where(r < n_rc - 1, next_row, 0.0)
        up = jnp.where(ridx == 0, prev_row, pltpu.roll(cur, 1, 0))
        dn = jnp.where(ridx == rows - 1, next_row, pltpu.roll(cur, rows - 1, 0))
        z = w0 * up + w1 * cur + w2 * dn + bias
        o_ref[0, pl.ds(base, rows), :] = _silu(z).astype(o_ref.dtype)
        return carry

    lax.fori_loop(0, n_rc, body, 0)


def _conv_silu(a3, conv_w, conv_b):
    bsz, s, _ = a3.shape
    c = conv_w.shape[1]
    cw = 256
    rows = min(256, s)
    return pl.pallas_call(
        functools.partial(_conv_kernel, rows=rows),
        grid=(bsz, c // cw),
        in_specs=[pl.BlockSpec((1, s, cw), lambda b, j: (b, 0, j)),
                  pl.BlockSpec((3, cw), lambda b, j: (0, j)),
                  pl.BlockSpec((1, cw), lambda b, j: (0, j))],
        out_specs=pl.BlockSpec((1, s, cw), lambda b, j: (b, 0, j)),
        out_shape=jax.ShapeDtypeStruct((bsz, s, c), BF16),
        compiler_params=_cparams("parallel", "parallel"),
        name="conv_silu",
    )(a3, conv_w, conv_b.reshape(1, c))


def _mlstm_kernel(q_ref, k_ref, v_ref, gc_ref, gr_ref, o_ref, c_ref, n_ref, m_ref, *, reverse):
    L = CHUNK_A

    @pl.when(pl.program_id(1) == 0)
    def _():
        c_ref[...] = jnp.zeros_like(c_ref)
        n_ref[...] = jnp.zeros_like(n_ref)
        m_ref[...] = jnp.zeros_like(m_ref)

    row = lax.broadcasted_iota(jnp.int32, (L, L), 0)
    col = lax.broadcasted_iota(jnp.int32, (L, L), 1)
    if reverse:
        mask = col >= row
        mask_t = row >= col
        last = 0
    else:
        mask = col <= row
        mask_t = row <= col
        last = L - 1
    m_mat = mask.astype(BF16)
    mt_mat = mask_t.astype(BF16)

    gc = gc_ref[0]
    gr = gr_ref[...]
    cum_c = sum(_dot(m_mat, p) for p in _split3(gc))
    cum_r = sum(_dot(p, mt_mat) for p in _split3(gr))
    off = H_A if reverse else 0
    scale = DH_A ** -0.5

    for h in range(H_A):
        ii = off + h
        fi = 2 * H_A + off + h
        hs = slice(h * DH_A, (h + 1) * DH_A)
        bc = cum_c[:, fi:fi + 1]
        br = cum_r[fi:fi + 1, :]
        ic = gc[:, ii:ii + 1]
        ir = gr[ii:ii + 1, :]
        m_st = m_ref[h:h + 1, 0:1]
        qh = q_ref[0, :, hs]
        kh = k_ref[0, :, hs]
        vh = v_ref[0, :, hs]
        c_st = c_ref[h]
        n_st = n_ref[h:h + 1, :]

        d_mat = jnp.where(mask, bc - br + ir, NEG)
        m_inter = bc + m_st
        m_row = jnp.maximum(m_inter, jnp.max(d_mat, axis=1, keepdims=True))
        w_inter = jnp.exp(m_inter - m_row)
        s_mat = _dot_nt(qh, kh) * scale * jnp.exp(d_mat - m_row)
        num = w_inter * _dot(qh, c_st.astype(BF16)) + _dot(s_mat.astype(BF16), vh)
        qn = jnp.sum(qh.astype(F32) * n_st, axis=1, keepdims=True)
        den = w_inter * qn + jnp.sum(s_mat, axis=1, keepdims=True)
        inv = 1.0 / jnp.maximum(jnp.abs(den), jnp.exp(-m_row))
        o_ref[0, :, hs] = num * inv

        b_last = bc[last:last + 1, :]
        g_end_r = b_last - br + ir
        g_end_c = b_last - bc + ic
        m_next = jnp.maximum(b_last + m_st, jnp.max(g_end_r, axis=1, keepdims=True))
        decay = jnp.exp(b_last + m_st - m_next)
        wk = jnp.exp(g_end_c - m_next) * (kh.astype(F32) * scale)
        c_ref[h] = decay * c_st + _dot_tn(wk.astype(BF16), vh)
        n_ref[h:h + 1, :] = decay * n_st + jnp.sum(wk, axis=0, keepdims=True)
        m_ref[h:h + 1, :] = jnp.broadcast_to(m_next, (1, m_ref.shape[1]))


def _mlstm_dir(qk3, a3, gc3, gr, reverse):
    bsz, s, _ = qk3.shape
    L = CHUNK_A
    nc = s // L
    cidx = (lambda c: nc - 1 - c) if reverse else (lambda c: c)
    return pl.pallas_call(
        functools.partial(_mlstm_kernel, reverse=reverse),
        grid=(bsz, nc),
        in_specs=[pl.BlockSpec((1, L, D_MODEL), lambda b, c: (b, cidx(c), 0)),
                  pl.BlockSpec((1, L, D_MODEL), lambda b, c: (b, cidx(c), 1)),
                  pl.BlockSpec((1, L, D_MODEL), lambda b, c: (b, cidx(c), 2)),
                  pl.BlockSpec((1, L, 4 * H_A), lambda b, c: (b, cidx(c), 0)),
                  pl.BlockSpec((4 * H_A, L), lambda b, c: (0, b * nc + cidx(c)))],
        out_specs=pl.BlockSpec((1, L, D_MODEL), lambda b, c: (b, cidx(c), 0)),
        out_shape=jax.ShapeDtypeStruct((bsz, s, D_MODEL), F32),
        scratch_shapes=[pltpu.VMEM((H_A, DH_A, DH_A), F32),
                        pltpu.VMEM((H_A, DH_A), F32),
                        pltpu.VMEM((H_A, 128), F32)],
        compiler_params=_cparams("parallel", "arbitrary"),
        name="mlstm_bwd" if reverse else "mlstm_fwd",
    )(qk3, qk3, a3, gc3, gr)


def _hgrn_kernel(q_ref, k_ref, v_ref, lf_ref, o_ref, st_ref, b_ref, q32_ref, k32_ref, v32_ref, *, reverse):
    L = CHUNK_B

    @pl.when(pl.program_id(1) == 0)
    def _():
        st_ref[...] = jnp.zeros_like(st_ref)

    row = lax.broadcasted_iota(jnp.int32, (L, L), 0)
    col = lax.broadcasted_iota(jnp.int32, (L, L), 1)
    if reverse:
        mask = col >= row
        last = 0
    else:
        mask = col <= row
        last = L - 1
    m_mat = mask.astype(BF16)

    b = sum(_dot(m_mat, p) for p in _split3(lf_ref[0]))
    q = q_ref[0].astype(F32)
    k = k_ref[0].astype(F32)
    v = v_ref[0]
    b_last = b[last:last + 1, :]
    q_in = (q * jnp.exp(b)).astype(BF16)
    k_out = (k * jnp.exp(b_last - b)).astype(BF16)
    dec = jnp.exp(b_last)

    for h in range(H_B):
        hs = slice(h * DK_B, (h + 1) * DK_B)
        st = st_ref[h]
        o_ref[0, :, hs] = _dot_nt(q_in[:, hs], st.astype(BF16))
        st_ref[h] = dec[:, hs] * st + _dot_tn(v[:, hs], k_out[:, hs])

    dev = b - b[L // 2:L // 2 + 1, :]
    rng = jnp.max(jnp.abs(dev))

    @pl.when(rng <= HGRN_SAFE_RANGE)
    def _():
        qt = (q * jnp.exp(dev)).astype(BF16)
        kt = (k * jnp.exp(-dev)).astype(BF16)
        for h in range(H_B):
            hs = slice(h * DK_B, (h + 1) * DK_B)
            a = jnp.where(mask, _dot_nt(qt[:, hs], kt[:, hs]), 0.0)
            o_ref[0, :, hs] += _dot(a.astype(BF16), v[:, hs])

    @pl.when(rng > HGRN_SAFE_RANGE)
    def _():
        b_ref[...] = b
        q32_ref[...] = q
        k32_ref[...] = k
        v32_ref[...] = v.astype(F32)
        sidx = lax.broadcasted_iota(jnp.int32, (L, 1), 0)

        def body(j, carry):
            is_j = sidx == j
            b_all = b_ref[...]
            bj = jnp.sum(jnp.where(is_j, b_all, 0.0), axis=0, keepdims=True)
            qj = jnp.sum(jnp.where(is_j, q32_ref[...], 0.0), axis=0, keepdims=True)
            valid = (sidx >= j) if reverse else (sidx <= j)
            e = jnp.exp(jnp.minimum(bj - b_all, 0.0))
            t = jnp.where(valid, k32_ref[...] * e * qj, 0.0)
            for h in range(H_B):
                hs = slice(h * DK_B, (h + 1) * DK_B)
                a = jnp.sum(t[:, hs], axis=1, keepdims=True)
                row_j = jnp.sum(a * v32_ref[:, hs], axis=0, keepdims=True)
                o_ref[0, :, hs] += jnp.where(is_j, row_j, 0.0)
            return carry

        lax.fori_loop(0, L, body, 0)


def _hgrn_dir(qs3, kf3, a3, lf3, reverse):
    bsz, s, _ = qs3.shape
    L = CHUNK_B
    nc = s // L
    w = D_MODEL
    cidx = (lambda c: nc - 1 - c) if reverse else (lambda c: c)
    d = 1 if reverse else 0
    return pl.pallas_call(
        functools.partial(_hgrn_kernel, reverse=reverse),
        grid=(bsz, nc),
        in_specs=[pl.BlockSpec((1, L, w), lambda b, c: (b, cidx(c), 0)),
                  pl.BlockSpec((1, L, w), lambda b, c: (b, cidx(c), d)),
                  pl.BlockSpec((1, L, w), lambda b, c: (b, cidx(c), 3)),
                  pl.BlockSpec((1, L, w), lambda b, c: (b, cidx(c), d))],
        out_specs=pl.BlockSpec((1, L, w), lambda b, c: (b, cidx(c), 0)),
        out_shape=jax.ShapeDtypeStruct((bsz, s, w), F32),
        scratch_shapes=[pltpu.VMEM((H_B, DK_B, DK_B), F32),
                        pltpu.VMEM((L, w), F32), pltpu.VMEM((L, w), F32),
                        pltpu.VMEM((L, w), F32), pltpu.VMEM((L, w), F32)],
        compiler_params=_cparams("parallel", "arbitrary"),
        name="hgrn_bwd" if reverse else "hgrn_fwd",
    )(qs3, kf3, a3, lf3)


def _head_norm(v, n_heads, center):
    width = v.shape[1] // n_heads
    parts = []
    for h in range(n_heads):
        seg = v[:, h * width:(h + 1) * width]
        if center:
            seg = seg - jnp.mean(seg, axis=1, keepdims=True)
        parts.append(seg * lax.rsqrt(jnp.mean(seg * seg, axis=1, keepdims=True) + EPS))
    return jnp.concatenate(parts, axis=1)


def _mix_kernel(x_ref, haf_ref, hab_ref, hbf_ref, hbb_ref, og_ref, ga_ref, gb_ref, sgb_ref,
                na_ref, nb_ref, woa_ref, wob_ref, wo_ref, g_ref, b_ref, o_ref, ob_ref, *, alpha):
    h_a = og_ref[...].astype(F32) * (_head_norm(haf_ref[...] + hab_ref[...], H_A, True) * na_ref[...])
    y_a = _dot(h_a.astype(BF16), woa_ref[...])
    h_b = (_head_norm(hbf_ref[...] + hbb_ref[...], H_B, False) * nb_ref[...]) * sgb_ref[...].astype(F32)
    y_b = _dot(h_b.astype(BF16), wob_ref[...])
    merged = ga_ref[...].astype(F32) * y_a + gb_ref[...].astype(F32) * y_b
    mix = _dot(merged.astype(BF16), wo_ref[...])
    y = _layer_norm_rows(alpha * x_ref[...] + mix, g_ref[...], b_ref[...])
    o_ref[...] = y
    ob_ref[...] = y.astype(BF16)


def _mix_call(x, haf, hab, hbf, hbb, sg, sl, norm_a, norm_b, woa, wob, wo, g, b, alpha):
    t, d = x.shape
    tm = min(256, t)
    tile = lambda j=0: pl.BlockSpec((tm, d), lambda i: (i, j))
    vec = lambda: pl.BlockSpec((1, d), lambda i: (0, 0))
    mat = lambda: pl.BlockSpec((d, d), lambda i: (0, 0))
    r = lambda a: a.reshape(1, d)
    return pl.pallas_call(
        functools.partial(_mix_kernel, alpha=alpha),
        grid=(t // tm,),
        in_specs=[tile(), tile(), tile(), tile(), tile(),
                  tile(0), tile(1), tile(2), tile(1),
                  vec(), vec(), mat(), mat(), mat(), vec(), vec()],
        out_specs=[tile(), tile()],
        out_shape=[jax.ShapeDtypeStruct((t, d), F32), jax.ShapeDtypeStruct((t, d), BF16)],
        compiler_params=_cparams("parallel"),
        name="mix_out_ln1",
    )(x, haf, hab, hbf, hbb, sg, sg, sg, sl, r(norm_a), r(norm_b), woa, wob, wo, r(g), r(b))


def _dense_kernel(xb_ref, p_ref, ws13_ref, ws2_ref, wple_ref, wpg_ref, o_ref):
    xb = xb_ref[...]
    h13 = _dot(xb, ws13_ref[...])
    hid = _silu(h13[:, :D_EXPERT]) * h13[:, D_EXPERT:]
    shared = _dot(hid.astype(BF16), ws2_ref[...])
    ple = _dot(p_ref[...].astype(BF16), wple_ref[...]) * jax.nn.sigmoid(_dot(xb, wpg_ref[...]))
    o_ref[...] = shared + ple


def _dense_call(xb, p, ws13, ws2, wple, wpg):
    t, d = xb.shape
    tm = min(512, t)
    full = lambda a: pl.BlockSpec(a.shape, lambda i: (0, 0))
    return pl.pallas_call(
        _dense_kernel,
        grid=(t // tm,),
        in_specs=[pl.BlockSpec((tm, d), lambda i: (i, 0)),
                  pl.BlockSpec((tm, PLE_DIM), lambda i: (i, 0)),
                  full(ws13), full(ws2), full(wple), full(wpg)],
        out_specs=pl.BlockSpec((tm, d), lambda i: (i, 0)),
        out_shape=jax.ShapeDtypeStruct((t, d), F32),
        compiler_params=_cparams("parallel"),
        name="shared_ple",
    )(xb, p, ws13, ws2, wple, wpg)


def _first_index(hit, iota, big):
    return jnp.min(jnp.where(hit, iota, big), axis=0, keepdims=True)


def _router_kernel(x_ref, wh_ref, wl_ref, bias_ref, idx_ref, wts_ref):
    x = x_ref[...]
    xh = x.astype(BF16)
    xl = (x - xh.astype(F32)).astype(BF16)
    wh = wh_ref[...]
    logits = _dot_nt(wh, xh) + _dot_nt(wh, xl) + _dot_nt(wl_ref[...], xh)
    scores = jax.nn.sigmoid(logits)
    sel = scores + bias_ref[...]
    tm = sel.shape[1]
    ninf = -jnp.inf

    gi = lax.broadcasted_iota(jnp.int32, (GROUP_SIZE, tm), 0).astype(F32)
    g_rows = []
    for g in range(N_GROUPS):
        seg = sel[g * GROUP_SIZE:(g + 1) * GROUP_SIZE, :]
        m1 = jnp.max(seg, axis=0, keepdims=True)
        first = _first_index(seg == m1, gi, float(GROUP_SIZE))
        m2 = jnp.max(jnp.where(gi == first, ninf, seg), axis=0, keepdims=True)
        g_rows.append(m1 + m2)
    g_score = jnp.concatenate(g_rows, axis=0)

    ggi = lax.broadcasted_iota(jnp.int32, (N_GROUPS, tm), 0).astype(F32)
    g_keep = jnp.zeros((N_GROUPS, tm), F32)
    cur = g_score
    for _ in range(TOPK_GROUPS):
        mx = jnp.max(cur, axis=0, keepdims=True)
        hit = ggi == _first_index(cur == mx, ggi, float(N_GROUPS))
        g_keep = jnp.where(hit, 1.0, g_keep)
        cur = jnp.where(hit, ninf, cur)

    keep = jnp.concatenate(
        [jnp.broadcast_to(g_keep[g:g + 1, :], (GROUP_SIZE, tm)) for g in range(N_GROUPS)], axis=0)
    cur = jnp.where(keep > 0.0, sel, NEG)
    ei = lax.broadcasted_iota(jnp.int32, (N_EXPERTS, tm), 0).astype(F32)
    idx_rows, w_rows = [], []
    for _ in range(TOP_K):
        mx = jnp.max(cur, axis=0, keepdims=True)
        first = _first_index(cur == mx, ei, float(N_EXPERTS))
        hit = ei == first
        idx_rows.append(first)
        w_rows.append(jnp.sum(jnp.where(hit, scores, 0.0), axis=0, keepdims=True))
        cur = jnp.where(hit, ninf, cur)
    w = jnp.concatenate(w_rows, axis=0)
    idx_ref[...] = jnp.concatenate(idx_rows, axis=0).astype(jnp.int32)
    wts_ref[...] = w / jnp.sum(w, axis=0, keepdims=True) * ROUTED_SCALE


def _router_call(x, w_router, router_bias):
    t, d = x.shape
    tm = min(512, t)
    wt = w_router.T
    wh = wt.astype(BF16)
    wl = (wt - wh.astype(F32)).astype(BF16)
    return pl.pallas_call(
        _router_kernel,
        grid=(t // tm,),
        in_specs=[pl.BlockSpec((tm, d), lambda i: (i, 0)),
                  pl.BlockSpec((N_EXPERTS, d), lambda i: (0, 0)),
                  pl.BlockSpec((N_EXPERTS, d), lambda i: (0, 0)),
                  pl.BlockSpec((N_EXPERTS, 1), lambda i: (0, 0))],
        out_specs=[pl.BlockSpec((TOP_K, tm), lambda i: (0, i)),
                   pl.BlockSpec((TOP_K, tm), lambda i: (0, i))],
        out_shape=[jax.ShapeDtypeStruct((TOP_K, t), jnp.int32),
                   jax.ShapeDtypeStruct((TOP_K, t), F32)],
        compiler_params=_cparams("parallel"),
        name="router",
    )(x, wh, wl, router_bias.astype(F32).reshape(N_EXPERTS, 1))


def _row_copy(src_hbm, dst, sem, tok, r):
    return pltpu.make_async_copy(src_hbm.at[pl.ds(tok, 1), :], dst.at[pl.ds(r, 1), :], sem)


def _expert_kernel(blk_e_ref, nused_ref, tok_ref, tokn_ref, x_hbm, w13_ref, w2_ref, o_ref, xbuf, sem):
    i = pl.program_id(0)
    n_used = nused_ref[0]
    bm = xbuf.shape[1]
    slot = i % 2

    def issue(t_ref, s):
        def body(r, carry):
            _row_copy(x_hbm, xbuf.at[s], sem.at[s], t_ref[0, 0, r], r).start()
            return carry
        lax.fori_loop(0, bm, body, 0)

    @pl.when(jnp.logical_and(i == 0, n_used > 0))
    def _():
        issue(tok_ref, 0)

    @pl.when(i + 1 < n_used)
    def _():
        issue(tokn_ref, 1 - slot)

    @pl.when(i < n_used)
    def _():
        pltpu.make_async_copy(x_hbm.at[pl.ds(0, bm), :], xbuf.at[slot], sem.at[slot]).wait()
        xb = xbuf[slot].astype(BF16)
        h13 = _dot(xb, w13_ref[0])
        hid = _silu(h13[:, :D_EXPERT]) * h13[:, D_EXPERT:]
        o_ref[...] = _dot(hid.astype(BF16), w2_ref[0])

    @pl.when(i >= n_used)
    def _():
        o_ref[...] = jnp.zeros_like(o_ref)


def _expert_call(x, blk_e, n_used, buf_tok, w13, w2):
    t, d = x.shape
    bm = MOE_BLOCK
    n_blk = blk_e.shape[0]
    tok3 = buf_tok.reshape(n_blk, 1, bm)
    grid_spec = pltpu.PrefetchScalarGridSpec(
        num_scalar_prefetch=2,
        grid=(n_blk,),
        in_specs=[pl.BlockSpec((1, 1, bm), lambda i, e, n: (i, 0, 0), memory_space=pltpu.SMEM),
                  pl.BlockSpec((1, 1, bm), lambda i, e, n: (jnp.minimum(i + 1, n_blk - 1), 0, 0),
                               memory_space=pltpu.SMEM),
                  pl.BlockSpec(memory_space=pl.ANY),
                  pl.BlockSpec((1, d, 2 * D_EXPERT), lambda i, e, n: (e[i], 0, 0)),
                  pl.BlockSpec((1, D_EXPERT, d), lambda i, e, n: (e[i], 0, 0))],
        out_specs=pl.BlockSpec((bm, d), lambda i, e, n: (i, 0)),
        scratch_shapes=[pltpu.VMEM((2, bm, d), F32), pltpu.SemaphoreType.DMA((2,))],
    )
    return pl.pallas_call(
        _expert_kernel,
        grid_spec=grid_spec,
        out_shape=jax.ShapeDtypeStruct((n_blk * bm, d), F32),
        compiler_params=_cparams("arbitrary"),
        name="routed_experts",
    )(blk_e, n_used, tok3, tok3, x, w13, w2)


def _combine_kernel(slot_ref, slotn_ref, ys_hbm, w_ref, dense_ref, x_ref, g_ref, b_ref,
                    o_ref, ob_ref, gbuf, sem, *, alpha):
    i = pl.program_id(0)
    n = pl.num_programs(0)
    rows = gbuf.shape[1]
    tm = rows // TOP_K
    cur = i % 2

    def issue(s_ref, s):
        def body(r, carry):
            _row_copy(ys_hbm, gbuf.at[s], sem.at[s], s_ref[0, 0, r], r).start()
            return carry
        lax.fori_loop(0, rows, body, 0)

    @pl.when(i == 0)
    def _():
        issue(slot_ref, 0)

    @pl.when(i + 1 < n)
    def _():
        issue(slotn_ref, 1 - cur)

    pltpu.make_async_copy(ys_hbm.at[pl.ds(0, rows), :], gbuf.at[cur], sem.at[cur]).wait()
    w = w_ref[...]
    routed = gbuf[cur, 0:tm, :] * w[:, 0:1]
    for k in range(1, TOP_K):
        routed = routed + gbuf[cur, k * tm:(k + 1) * tm, :] * w[:, k:k + 1]
    y = _layer_norm_rows(alpha * x_ref[...] + routed + dense_ref[...], g_ref[...], b_ref[...])
    o_ref[...] = y
    ob_ref[...] = y.astype(BF16)


def _combine_call(ys, slots, wts, dense, x, g, b, alpha):
    t, d = x.shape
    tm = min(COMBINE_TILE, t)
    nt = t // tm
    rows = tm * TOP_K
    slot3 = slots.reshape(nt, tm, TOP_K).transpose(0, 2, 1).reshape(nt, 1, rows)
    tile = lambda: pl.BlockSpec((tm, d), lambda i: (i, 0))
    vec = lambda: pl.BlockSpec((1, d), lambda i: (0, 0))
    return pl.pallas_call(
        functools.partial(_combine_kernel, alpha=alpha),
        grid=(nt,),
        in_specs=[pl.BlockSpec((1, 1, rows), lambda i: (i, 0, 0), memory_space=pltpu.SMEM),
                  pl.BlockSpec((1, 1, rows), lambda i: (jnp.minimum(i + 1, nt - 1), 0, 0),
                               memory_space=pltpu.SMEM),
                  pl.BlockSpec(memory_space=pl.ANY),
                  pl.BlockSpec((tm, TOP_K), lambda i: (i, 0)),
                  tile(), tile(), vec(), vec()],
        out_specs=[tile(), tile()],
        out_shape=[jax.ShapeDtypeStruct((t, d), F32), jax.ShapeDtypeStruct((t, d), BF16)],
        scratch_shapes=[pltpu.VMEM((2, rows, d), F32), pltpu.SemaphoreType.DMA((2,))],
        compiler_params=_cparams("arbitrary"),
        name="combine_ln2",
    )(slot3, slot3, ys, wts, dense, x, g.reshape(1, d), b.reshape(1, d))


def _dispatch_tables(idx_kt):
    bm = MOE_BLOCK
    t = idx_kt.shape[1]
    n_asg = t * TOP_K
    n_blk = n_asg // bm + N_EXPERTS
    flat_e = idx_kt.T.reshape(-1)
    order = jnp.argsort(flat_e).astype(jnp.int32)
    e_sorted = flat_e[order]
    counts = jnp.bincount(flat_e, length=N_EXPERTS).astype(jnp.int32)
    padded = (counts + bm - 1) // bm * bm
    p_end = jnp.cumsum(padded)
    p_start = p_end - padded
    start = jnp.cumsum(counts) - counts
    dest = (p_start[e_sorted] + jnp.arange(n_asg, dtype=jnp.int32) - start[e_sorted]).astype(jnp.int32)
    buf_tok = jnp.zeros((n_blk * bm,), jnp.int32).at[dest].set(order // TOP_K)
    slots = jnp.zeros((n_asg,), jnp.int32).at[order].set(dest).reshape(t, TOP_K)
    blk_e = jnp.minimum(jnp.searchsorted(p_end, jnp.arange(n_blk, dtype=jnp.int32) * bm, side='right'),
                        N_EXPERTS - 1).astype(jnp.int32)
    n_used = (p_end[-1] // bm).astype(jnp.int32).reshape(1)
    return buf_tok, slots, blk_e, n_used


def _prep_layer(li, lower, w_in, b_in, conv_w, conv_b, norm_a, norm_b, w_out_a, w_out_b, w_o,
                ln1_g, ln1_b, w_ple, w_ple_gate, w_router, router_bias, w1, w3, w2, ws1, ws3, ws2,
                ln2_g, ln2_b):
    d = D_MODEL
    w = w_in[li]
    bi = b_in[li]
    seg = lambda a, lo, hi: a[..., lo:hi]
    o_qkv, o_oa, o_g = 0, 3 * d, 4 * d
    o_qb = 4 * d + 4 * H_A
    o_ff, o_ib, o_gb, o_ga = o_qb + d, o_qb + 3 * d, o_qb + 4 * d, o_qb + 5 * d
    cat = lambda parts: jnp.concatenate(parts, axis=-1)
    raw_cols = [(o_qkv, o_qkv + 3 * d), (o_ib, o_ib + d)]
    sig_cols = [(o_oa, o_oa + d), (o_ga, o_ga + 2 * d)]
    silu_cols = [(o_qb, o_qb + d), (o_gb, o_gb + d)]
    return dict(
        w_raw=cat([seg(w, *c) for c in raw_cols]).astype(BF16), b_raw=cat([seg(bi, *c) for c in raw_cols]),
        w_sig=cat([seg(w, *c) for c in sig_cols]).astype(BF16), b_sig=cat([seg(bi, *c) for c in sig_cols]),
        w_silu=cat([seg(w, *c) for c in silu_cols]).astype(BF16), b_silu=cat([seg(bi, *c) for c in silu_cols]),
        w_fg=seg(w, o_ff, o_ff + 2 * d).astype(BF16), b_fg=seg(bi, o_ff, o_ff + 2 * d),
        w_g=seg(w, o_g, o_g + 4 * H_A).astype(BF16), b_g=seg(bi, o_g, o_g + 4 * H_A),
        lower=lower[li].reshape(-1),
        conv_w=conv_w[li], conv_b=conv_b[li], norm_a=norm_a[li], norm_b=norm_b[li],
        w_out_a=w_out_a[li].astype(BF16), w_out_b=w_out_b[li].astype(BF16), w_o=w_o[li].astype(BF16),
        ln1_g=ln1_g[li], ln1_b=ln1_b[li],
        w_ple=w_ple[li].astype(BF16), w_ple_gate=w_ple_gate[li].astype(BF16),
        w_router=w_router[li], router_bias=router_bias[li],
        w13=jnp.concatenate([w1[li], w3[li]], axis=-1).astype(BF16), w2=w2[li].astype(BF16),
        ws13=jnp.concatenate([ws1[li], ws3[li]], axis=-1).astype(BF16), ws2=ws2[li].astype(BF16),
        ln2_g=ln2_g[li], ln2_b=ln2_b[li],
    )


def _layer(x, xb, p, lp, bsz, s, alpha):
    t, d = x.shape
    a = _proj_act(xb, lp["w_raw"], lp["b_raw"], "raw", "proj_raw")
    sg = _proj_act(xb, lp["w_sig"], lp["b_sig"], "sigmoid", "proj_sigmoid")
    sl = _proj_act(xb, lp["w_silu"], lp["b_silu"], "silu", "proj_silu")
    kf, lf = _proj_forget(xb, lp["w_fg"], lp["b_fg"], lp["lower"])
    gc, gr = _proj_gates(xb, lp["w_g"], lp["b_g"])

    a3 = a.reshape(bsz, s, 4 * d)
    qk3 = _conv_silu(a3, lp["conv_w"], lp["conv_b"])
    gc3 = gc.reshape(bsz, s, 4 * H_A)
    haf = _mlstm_dir(qk3, a3, gc3, gr, False).reshape(t, d)
    hab = _mlstm_dir(qk3, a3, gc3, gr, True).reshape(t, d)

    sl3 = sl.reshape(bsz, s, 2 * d)
    kf3 = kf.reshape(bsz, s, 2 * d)
    lf3 = lf.reshape(bsz, s, 2 * d)
    hbf = _hgrn_dir(sl3, kf3, a3, lf3, False).reshape(t, d)
    hbb = _hgrn_dir(sl3, kf3, a3, lf3, True).reshape(t, d)

    x1, x1b = _mix_call(x, haf, hab, hbf, hbb, sg, sl, lp["norm_a"], lp["norm_b"],
                        lp["w_out_a"], lp["w_out_b"], lp["w_o"], lp["ln1_g"], lp["ln1_b"], alpha)

    dense = _dense_call(x1b, p, lp["ws13"], lp["ws2"], lp["w_ple"], lp["w_ple_gate"])
    idx_kt, wts_kt = _router_call(x1, lp["w_router"], lp["router_bias"])
    buf_tok, slots, blk_e, n_used = _dispatch_tables(idx_kt)
    ys = _expert_call(x1, blk_e, n_used, buf_tok, lp["w13"], lp["w2"])
    return _combine_call(ys, slots, wts_kt.T, dense, x1, lp["ln2_g"], lp["ln2_b"], alpha)


def kernel(x_prompt, x_sample, p_prompt, p_sample, emb_ln_g, emb_ln_b, w_in, b_in, conv_w, conv_b, norm_a, norm_b, lb_param, w_out_a, w_out_b, w_o, ln1_g, ln1_b, w_ple, w_ple_gate, w_router, router_bias, w1, w3, w2, ws1, ws3, ws2, ln2_g, ln2_b):
    depth = w_in.shape[0]
    alpha = (2.0 * depth) ** 0.25
    lb_soft = jax.nn.softmax(lb_param.astype(F32), axis=0)
    lower = jnp.cumsum(lb_soft, axis=0) - lb_soft[0:1]
    layers = [_prep_layer(li, lower, w_in, b_in, conv_w, conv_b, norm_a, norm_b, w_out_a, w_out_b, w_o,
                          ln1_g, ln1_b, w_ple, w_ple_gate, w_router, router_bias, w1, w3, w2,
                          ws1, ws3, ws2, ln2_g, ln2_b) for li in range(depth)]

    def trunk(x3, p4):
        bsz, s, d = x3.shape
        x, xb = _layer_norm_call(x3.reshape(bsz * s, d), emb_ln_g, emb_ln_b)
        for li in range(depth):
            x, xb = _layer(x, xb, p4[li].reshape(bsz * s, -1), layers[li], bsz, s, alpha)
        return x.reshape(bsz, s, d)

    return (trunk(x_prompt, p_prompt), trunk(x_sample, p_sample))
```

```python
import functools

import jax
import jax.numpy as jnp
from jax import lax
from jax.experimental import pallas as pl
from jax.experimental.pallas import tpu as pltpu

F32 = jnp.float32
BF16 = jnp.bfloat16

D_MODEL = 1024
PLE_DIM = 256
H_A = 4
DH_A = D_MODEL // H_A
CHUNK_A = 128
H_B = 8
DK_B = 128
CHUNK_B = 64
N_EXPERTS = 256
TOP_K = 8
N_GROUPS = 8
GROUP_SIZE = N_EXPERTS // N_GROUPS
TOPK_GROUPS = 4
D_EXPERT = 256
ROUTED_SCALE = 2.5
EPS = 1e-5
NEG = -1e30
TINY = 1e-30

HGRN_SAFE_RANGE = 80.0
HGRN_ROWS = 2 * CHUNK_B
MOE_BLOCK = 256
COMBINE_TILE = 128
VMEM_LIMIT = 56 * 1024 * 1024


def _cparams(*sem):
    return pltpu.CompilerParams(dimension_semantics=sem, vmem_limit_bytes=VMEM_LIMIT)


def _split3(x):
    hi = x.astype(BF16)
    r1 = x - hi.astype(F32)
    mid = r1.astype(BF16)
    lo = (r1 - mid.astype(F32)).astype(BF16)
    return hi, mid, lo


def _dot(a, b):
    return jnp.dot(a, b, preferred_element_type=F32)


def _dot_nt(a, b):
    return lax.dot_general(a, b, (((1,), (1,)), ((), ())), preferred_element_type=F32)


def _dot_tn(a, b):
    return lax.dot_general(a, b, (((0,), (0,)), ((), ())), preferred_element_type=F32)


def _layer_norm_rows(v, g, b):
    mu = jnp.mean(v, axis=-1, keepdims=True)
    vc = v - mu
    var = jnp.mean(vc * vc, axis=-1, keepdims=True)
    return vc * lax.rsqrt(var + EPS) * g + b


def _log_sigmoid(z):
    return jnp.minimum(z, 0.0) - jnp.log1p(jnp.exp(-jnp.abs(z)))


def _silu(z):
    return z * jax.nn.sigmoid(z)


def _ln_kernel(x_ref, g_ref, b_ref, o_ref, ob_ref):
    y = _layer_norm_rows(x_ref[...], g_ref[...], b_ref[...])
    o_ref[...] = y
    ob_ref[...] = y.astype(BF16)


def _layer_norm_call(x, g, b):
    t, d = x.shape
    tm = min(512, t)
    return pl.pallas_call(
        _ln_kernel,
        grid=(t // tm,),
        in_specs=[pl.BlockSpec((tm, d), lambda i: (i, 0)),
                  pl.BlockSpec((1, d), lambda i: (0, 0)),
                  pl.BlockSpec((1, d), lambda i: (0, 0))],
        out_specs=[pl.BlockSpec((tm, d), lambda i: (i, 0)),
                   pl.BlockSpec((tm, d), lambda i: (i, 0))],
        out_shape=[jax.ShapeDtypeStruct((t, d), F32), jax.ShapeDtypeStruct((t, d), BF16)],
        compiler_params=_cparams("parallel"),
        name="emb_layer_norm",
    )(x, g.reshape(1, d), b.reshape(1, d))


PROJ_SUB = 256


def _row_blocks(n):
    step = min(PROJ_SUB, n)
    return [slice(r, r + step) for r in range(0, n, step)]


def _proj_act_kernel(x_ref, w_ref, b_ref, o_ref, *, act):
    for rs in _row_blocks(x_ref.shape[0]):
        z = _dot(x_ref[rs, :], w_ref[...]) + b_ref[...]
        if act == "sigmoid":
            z = jax.nn.sigmoid(z)
        elif act == "silu":
            z = _silu(z)
        o_ref[rs, :] = z.astype(o_ref.dtype)


def _proj_act(xb, w, b, act, name):
    t, d = xb.shape
    n = w.shape[1]
    tm = min(1024, t)
    tn = 1024
    return pl.pallas_call(
        functools.partial(_proj_act_kernel, act=act),
        grid=(t // tm, n // tn),
        in_specs=[pl.BlockSpec((tm, d), lambda i, j: (i, 0)),
                  pl.BlockSpec((d, tn), lambda i, j: (0, j)),
                  pl.BlockSpec((1, tn), lambda i, j: (0, j))],
        out_specs=pl.BlockSpec((tm, tn), lambda i, j: (i, j)),
        out_shape=jax.ShapeDtypeStruct((t, n), BF16),
        compiler_params=_cparams("parallel", "arbitrary"),
        name=name,
    )(xb, w, b.reshape(1, n))


def _proj_forget_kernel(x_ref, w_ref, b_ref, oml_ref, la_ref, lc_ref, k_ref, lf_ref):
    a = la_ref[...]
    for rs in _row_blocks(x_ref.shape[0]):
        z = _dot(x_ref[rs, :], w_ref[...]) + b_ref[...]
        k_ref[rs, :] = (oml_ref[...] * jax.nn.sigmoid(-z)).astype(BF16)
        u = lc_ref[...] + _log_sigmoid(z)
        lf_ref[rs, :] = jnp.maximum(a, u) + jnp.log1p(jnp.exp(-jnp.abs(a - u)))


def _proj_forget(xb, w, b, lower):
    t, d = xb.shape
    n = w.shape[1]
    tm = min(1024, t)
    tn = 1024
    lb = lower.reshape(1, n)
    vec = lambda: pl.BlockSpec((1, tn), lambda i, j: (0, j))
    return pl.pallas_call(
        _proj_forget_kernel,
        grid=(t // tm, n // tn),
        in_specs=[pl.BlockSpec((tm, d), lambda i, j: (i, 0)),
                  pl.BlockSpec((d, tn), lambda i, j: (0, j)),
                  vec(), vec(), vec(), vec()],
        out_specs=[pl.BlockSpec((tm, tn), lambda i, j: (i, j)),
                   pl.BlockSpec((tm, tn), lambda i, j: (i, j))],
        out_shape=[jax.ShapeDtypeStruct((t, n), BF16), jax.ShapeDtypeStruct((t, n), F32)],
        compiler_params=_cparams("parallel", "arbitrary"),
        name="proj_forget",
    )(xb, w, b.reshape(1, n), 1.0 - lb, jnp.log(jnp.maximum(lb, TINY)), jnp.log1p(-lb))


def _proj_gates_kernel(x_ref, w_ref, wt_ref, b_ref, bt_ref, gc_ref, gr_ref):
    x = x_ref[...]
    zc = _dot(x, w_ref[...]) + b_ref[...]
    lane = lax.broadcasted_iota(jnp.int32, zc.shape, 1)
    gc_ref[...] = jnp.where(lane >= 2 * H_A, _log_sigmoid(zc), zc)
    zr = _dot_nt(wt_ref[...], x) + bt_ref[...]
    sub = lax.broadcasted_iota(jnp.int32, zr.shape, 0)
    gr_ref[...] = jnp.where(sub >= 2 * H_A, _log_sigmoid(zr), zr)


def _proj_gates(xb, w, b):
    t, d = xb.shape
    n = w.shape[1]
    tm = min(1024, t)
    return pl.pallas_call(
        _proj_gates_kernel,
        grid=(t // tm,),
        in_specs=[pl.BlockSpec((tm, d), lambda i: (i, 0)),
                  pl.BlockSpec((d, n), lambda i: (0, 0)),
                  pl.BlockSpec((n, d), lambda i: (0, 0)),
                  pl.BlockSpec((1, n), lambda i: (0, 0)),
                  pl.BlockSpec((n, 1), lambda i: (0, 0))],
        out_specs=[pl.BlockSpec((tm, n), lambda i: (i, 0)),
                   pl.BlockSpec((n, tm), lambda i: (0, i))],
        out_shape=[jax.ShapeDtypeStruct((t, n), F32), jax.ShapeDtypeStruct((n, t), F32)],
        compiler_params=_cparams("parallel"),
        name="proj_gates",
    )(xb, w, w.T, b.reshape(1, n), b.reshape(n, 1))


def _conv_kernel(x_ref, w_ref, b_ref, o_ref, *, rows):
    s = x_ref.shape[1]
    n_rc = s // rows
    w = w_ref[...]
    w0, w1, w2 = w[0:1, :], w[1:2, :], w[2:3, :]
    bias = b_ref[...]
    ridx = lax.broadcasted_iota(jnp.int32, (rows, x_ref.shape[2]), 0)

    def body(r, carry):
        base = pl.multiple_of(r * rows, rows)
        cur = x_ref[0, pl.ds(base, rows), :].astype(F32)
        pstart = pl.multiple_of(jnp.maximum(base - 16, 0), 16)
        nstart = pl.multiple_of(jnp.minimum(base + rows, s - 16), 16)
        prev_row = x_ref[0, pl.ds(pstart, 16), :][15:16, :].astype(F32)
        next_row = x_ref[0, pl.ds(nstart, 16), :][0:1, :].astype(F32)
        prev_row = jnp.where(r > 0, prev_row, 0.0)
        next_row = jnp.where(r < n_rc - 1, next_row, 0.0)
        up = jnp.where(ridx == 0, prev_row, pltpu.roll(cur, 1, 0))
        dn = jnp.where(ridx == rows - 1, next_row, pltpu.roll(cur, rows - 1, 0))
        z = w0 * up + w1 * cur + w2 * dn + bias
        o_ref[0, pl.ds(base, rows), :] = _silu(z).astype(o_ref.dtype)
        return carry

    lax.fori_loop(0, n_rc, body, 0)


def _conv_silu(a3, conv_w, conv_b):
    bsz, s, _ = a3.shape
    c = conv_w.shape[1]
    cw = 256
    rows = min(256, s)
    return pl.pallas_call(
        functools.partial(_conv_kernel, rows=rows),
        grid=(bsz, c // cw),
        in_specs=[pl.BlockSpec((1, s, cw), lambda b, j: (b, 0, j)),
                  pl.BlockSpec((3, cw), lambda b, j: (0, j)),
                  pl.BlockSpec((1, cw), lambda b, j: (0, j))],
        out_specs=pl.BlockSpec((1, s, cw), lambda b, j: (b, 0, j)),
        out_shape=jax.ShapeDtypeStruct((bsz, s, c), BF16),
        compiler_params=_cparams("parallel", "parallel"),
        name="conv_silu",
    )(a3, conv_w, conv_b.reshape(1, c))


def _mlstm_kernel(q_ref, k_ref, v_ref, gc_ref, gr_ref, o_ref, c_ref, n_ref, m_ref, *, reverse):
    L = CHUNK_A

    @pl.when(pl.program_id(1) == 0)
    def _():
        c_ref[...] = jnp.zeros_like(c_ref)
        n_ref[...] = jnp.zeros_like(n_ref)
        m_ref[...] = jnp.zeros_like(m_ref)

    row = lax.broadcasted_iota(jnp.int32, (L, L), 0)
    col = lax.broadcasted_iota(jnp.int32, (L, L), 1)
    if reverse:
        mask = col >= row
        mask_t = row >= col
        last = 0
    else:
        mask = col <= row
        mask_t = row <= col
        last = L - 1
    m_mat = mask.astype(BF16)
    mt_mat = mask_t.astype(BF16)

    gc = gc_ref[0]
    gr = gr_ref[...]
    cum_c = sum(_dot(m_mat, p) for p in _split3(gc))
    cum_r = sum(_dot(p, mt_mat) for p in _split3(gr))
    off = H_A if reverse else 0
    scale = DH_A ** -0.5

    for h in range(H_A):
        ii = off + h
        fi = 2 * H_A + off + h
        hs = slice(h * DH_A, (h + 1) * DH_A)
        bc = cum_c[:, fi:fi + 1]
        br = cum_r[fi:fi + 1, :]
        ic = gc[:, ii:ii + 1]
        ir = gr[ii:ii + 1, :]
        m_st = m_ref[h:h + 1, 0:1]
        qh = q_ref[0, :, hs]
        kh = k_ref[0, :, hs]
        vh = v_ref[0, :, hs]
        c_st = c_ref[h]
        n_st = n_ref[h:h + 1, :]

        d_mat = jnp.where(mask, bc - br + ir, NEG)
        m_inter = bc + m_st
        m_row = jnp.maximum(m_inter, jnp.max(d_mat, axis=1, keepdims=True))
        w_inter = jnp.exp(m_inter - m_row)
        s_mat = _dot_nt(qh, kh) * scale * jnp.exp(d_mat - m_row)
        num = w_inter * _dot(qh, c_st.astype(BF16)) + _dot(s_mat.astype(BF16), vh)
        qn = jnp.sum(qh.astype(F32) * n_st, axis=1, keepdims=True)
        den = w_inter * qn + jnp.sum(s_mat, axis=1, keepdims=True)
        inv = 1.0 / jnp.maximum(jnp.abs(den), jnp.exp(-m_row))
        o_ref[0, :, hs] = num * inv

        b_last = bc[last:last + 1, :]
        g_end_r = b_last - br + ir
        g_end_c = b_last - bc + ic
        m_next = jnp.maximum(b_last + m_st, jnp.max(g_end_r, axis=1, keepdims=True))
        decay = jnp.exp(b_last + m_st - m_next)
        wk = jnp.exp(g_end_c - m_next) * (kh.astype(F32) * scale)
        c_ref[h] = decay * c_st + _dot_tn(wk.astype(BF16), vh)
        n_ref[h:h + 1, :] = decay * n_st + jnp.sum(wk, axis=0, keepdims=True)
        m_ref[h:h + 1, :] = jnp.broadcast_to(m_next, (1, m_ref.shape[1]))


def _mlstm_dir(qk3, a3, gc3, gr, reverse):
    bsz, s, _ = qk3.shape
    L = CHUNK_A
    nc = s // L
    cidx = (lambda c: nc - 1 - c) if reverse else (lambda c: c)
    return pl.pallas_call(
        functools.partial(_mlstm_kernel, reverse=reverse),
        grid=(bsz, nc),
        in_specs=[pl.BlockSpec((1, L, D_MODEL), lambda b, c: (b, cidx(c), 0)),
                  pl.BlockSpec((1, L, D_MODEL), lambda b, c: (b, cidx(c), 1)),
                  pl.BlockSpec((1, L, D_MODEL), lambda b, c: (b, cidx(c), 2)),
                  pl.BlockSpec((1, L, 4 * H_A), lambda b, c: (b, cidx(c), 0)),
                  pl.BlockSpec((4 * H_A, L), lambda b, c: (0, b * nc + cidx(c)))],
        out_specs=pl.BlockSpec((1, L, D_MODEL), lambda b, c: (b, cidx(c), 0)),
        out_shape=jax.ShapeDtypeStruct((bsz, s, D_MODEL), F32),
        scratch_shapes=[pltpu.VMEM((H_A, DH_A, DH_A), F32),
                        pltpu.VMEM((H_A, DH_A), F32),
                        pltpu.VMEM((H_A, 128), F32)],
        compiler_params=_cparams("parallel", "arbitrary"),
        name="mlstm_bwd" if reverse else "mlstm_fwd",
    )(qk3, qk3, a3, gc3, gr)


def _hgrn_kernel(q_ref, k_ref, v_ref, lf_ref, o_ref, st_ref, oi_ref, b_ref, q32_ref, k32_ref, v32_ref, *,
                 reverse):
    L = CHUNK_B
    rows_blk = q_ref.shape[1]
    n_sub = rows_blk // L
    width = q_ref.shape[2]

    @pl.when(pl.program_id(1) == 0)
    def _():
        st_ref[...] = jnp.zeros_like(st_ref)

    row = lax.broadcasted_iota(jnp.int32, (rows_blk, rows_blk), 0)
    col = lax.broadcasted_iota(jnp.int32, (rows_blk, rows_blk), 1)
    vis = (col >= row) if reverse else (col <= row)
    same = (row // L) == (col // L)
    last = 0 if reverse else L - 1
    mask = vis[0:L, 0:L]

    m_all = jnp.logical_and(vis, same).astype(BF16)
    b = sum(_dot(m_all, p) for p in _split3(lf_ref[0]))

    def per_chunk_row(r):
        return jnp.concatenate(
            [jnp.broadcast_to(b[s * L + r:s * L + r + 1, :], (L, width)) for s in range(n_sub)], axis=0)

    q = q_ref[0].astype(F32)
    k = k_ref[0].astype(F32)
    v = v_ref[0]
    q_in = (q * jnp.exp(b)).astype(BF16)
    k_out = (k * jnp.exp(per_chunk_row(last) - b)).astype(BF16)
    dev = b - per_chunk_row(L // 2)
    rng = jnp.max(jnp.abs(dev))
    qt = (q * jnp.exp(dev)).astype(BF16)
    kt = (k * jnp.exp(-dev)).astype(BF16)

    for s in (reversed(range(n_sub)) if reverse else range(n_sub)):
        rs = slice(s * L, (s + 1) * L)
        dec = jnp.exp(b[s * L + last:s * L + last + 1, :])
        for h in range(H_B):
            hs = slice(h * DK_B, (h + 1) * DK_B)
            st = st_ref[h]
            o_inter = _dot_nt(q_in[rs, hs], st.astype(BF16))
            a = jnp.where(mask, _dot_nt(qt[rs, hs], kt[rs, hs]), 0.0)
            o_ref[0, rs, hs] = o_inter + _dot(a.astype(BF16), v[rs, hs])
            oi_ref[rs, hs] = o_inter
            st_ref[h] = dec[:, hs] * st + _dot_tn(v[rs, hs], k_out[rs, hs])

    @pl.when(rng > HGRN_SAFE_RANGE)
    def _():
        b_ref[...] = b
        q32_ref[...] = q
        k32_ref[...] = k
        v32_ref[...] = v.astype(F32)
        o_ref[0] = oi_ref[...]
        sidx = lax.broadcasted_iota(jnp.int32, (rows_blk, 1), 0)

        def body(j, carry):
            is_j = sidx == j
            b_all = b_ref[...]
            bj = jnp.sum(jnp.where(is_j, b_all, 0.0), axis=0, keepdims=True)
            qj = jnp.sum(jnp.where(is_j, q32_ref[...], 0.0), axis=0, keepdims=True)
            valid = jnp.logical_and((sidx >= j) if reverse else (sidx <= j), sidx // L == j // L)
            e = jnp.exp(jnp.minimum(bj - b_all, 0.0))
            t = jnp.where(valid, k32_ref[...] * e * qj, 0.0)
            for h in range(H_B):
                hs = slice(h * DK_B, (h + 1) * DK_B)
                a = jnp.sum(t[:, hs], axis=1, keepdims=True)
                row_j = jnp.sum(a * v32_ref[:, hs], axis=0, keepdims=True)
                o_ref[0, :, hs] += jnp.where(is_j, row_j, 0.0)
            return carry

        lax.fori_loop(0, rows_blk, body, 0)


def _hgrn_dir(qs3, kf3, a3, lf3, reverse):
    bsz, s, _ = qs3.shape
    L = HGRN_ROWS
    nc = s // L
    w = D_MODEL
    cidx = (lambda c: nc - 1 - c) if reverse else (lambda c: c)
    d = 1 if reverse else 0
    return pl.pallas_call(
        functools.partial(_hgrn_kernel, reverse=reverse),
        grid=(bsz, nc),
        in_specs=[pl.BlockSpec((1, L, w), lambda b, c: (b, cidx(c), 0)),
                  pl.BlockSpec((1, L, w), lambda b, c: (b, cidx(c), d)),
                  pl.BlockSpec((1, L, w), lambda b, c: (b, cidx(c), 3)),
                  pl.BlockSpec((1, L, w), lambda b, c: (b, cidx(c), d))],
        out_specs=pl.BlockSpec((1, L, w), lambda b, c: (b, cidx(c), 0)),
        out_shape=jax.ShapeDtypeStruct((bsz, s, w), F32),
        scratch_shapes=[pltpu.VMEM((H_B, DK_B, DK_B), F32)] + [pltpu.VMEM((L, w), F32)] * 5,
        compiler_params=_cparams("parallel", "arbitrary"),
        name="hgrn_bwd" if reverse else "hgrn_fwd",
    )(qs3, kf3, a3, lf3)


def _head_norm(v, n_heads, center):
    width = v.shape[1] // n_heads
    parts = []
    for h in range(n_heads):
        seg = v[:, h * width:(h + 1) * width]
        if center:
            seg = seg - jnp.mean(seg, axis=1, keepdims=True)
        parts.append(seg * lax.rsqrt(jnp.mean(seg * seg, axis=1, keepdims=True) + EPS))
    return jnp.concatenate(parts, axis=1)


def _mix_kernel(x_ref, haf_ref, hab_ref, hbf_ref, hbb_ref, og_ref, ga_ref, gb_ref, sgb_ref,
                na_ref, nb_ref, woa_ref, wob_ref, wo_ref, g_ref, b_ref, o_ref, ob_ref, *, alpha):
    h_a = og_ref[...].astype(F32) * (_head_norm(haf_ref[...] + hab_ref[...], H_A, True) * na_ref[...])
    y_a = _dot(h_a.astype(BF16), woa_ref[...])
    h_b = (_head_norm(hbf_ref[...] + hbb_ref[...], H_B, False) * nb_ref[...]) * sgb_ref[...].astype(F32)
    y_b = _dot(h_b.astype(BF16), wob_ref[...])
    merged = ga_ref[...].astype(F32) * y_a + gb_ref[...].astype(F32) * y_b
    mix = _dot(merged.astype(BF16), wo_ref[...])
    y = _layer_norm_rows(alpha * x_ref[...] + mix, g_ref[...], b_ref[...])
    o_ref[...] = y
    ob_ref[...] = y.astype(BF16)


def _mix_call(x, haf, hab, hbf, hbb, sg, sl, norm_a, norm_b, woa, wob, wo, g, b, alpha):
    t, d = x.shape
    tm = min(256, t)
    tile = lambda j=0: pl.BlockSpec((tm, d), lambda i: (i, j))
    vec = lambda: pl.BlockSpec((1, d), lambda i: (0, 0))
    mat = lambda: pl.BlockSpec((d, d), lambda i: (0, 0))
    r = lambda a: a.reshape(1, d)
    return pl.pallas_call(
        functools.partial(_mix_kernel, alpha=alpha),
        grid=(t // tm,),
        in_specs=[tile(), tile(), tile(), tile(), tile(),
                  tile(0), tile(1), tile(2), tile(1),
                  vec(), vec(), mat(), mat(), mat(), vec(), vec()],
        out_specs=[tile(), tile()],
        out_shape=[jax.ShapeDtypeStruct((t, d), F32), jax.ShapeDtypeStruct((t, d), BF16)],
        compiler_params=_cparams("parallel"),
        name="mix_out_ln1",
    )(x, haf, hab, hbf, hbb, sg, sg, sg, sl, r(norm_a), r(norm_b), woa, wob, wo, r(g), r(b))


def _dense_kernel(xb_ref, p_ref, ws13_ref, ws2_ref, wple_ref, wpg_ref, o_ref):
    xb = xb_ref[...]
    h13 = _dot(xb, ws13_ref[...])
    hid = _silu(h13[:, :D_EXPERT]) * h13[:, D_EXPERT:]
    shared = _dot(hid.astype(BF16), ws2_ref[...])
    ple = _dot(p_ref[...].astype(BF16), wple_ref[...]) * jax.nn.sigmoid(_dot(xb, wpg_ref[...]))
    o_ref[...] = shared + ple


def _dense_call(xb, p, ws13, ws2, wple, wpg):
    t, d = xb.shape
    tm = min(512, t)
    full = lambda a: pl.BlockSpec(a.shape, lambda i: (0, 0))
    return pl.pallas_call(
        _dense_kernel,
        grid=(t // tm,),
        in_specs=[pl.BlockSpec((tm, d), lambda i: (i, 0)),
                  pl.BlockSpec((tm, PLE_DIM), lambda i: (i, 0)),
                  full(ws13), full(ws2), full(wple), full(wpg)],
        out_specs=pl.BlockSpec((tm, d), lambda i: (i, 0)),
        out_shape=jax.ShapeDtypeStruct((t, d), F32),
        compiler_params=_cparams("parallel"),
        name="shared_ple",
    )(xb, p, ws13, ws2, wple, wpg)


def _first_index(hit, iota, big):
    return jnp.min(jnp.where(hit, iota, big), axis=0, keepdims=True)


def _router_kernel(x_ref, wh_ref, wl_ref, bias_ref, idx_ref, wts_ref, pos_ref, cnt_ref, run_ref):
    @pl.when(pl.program_id(0) == 0)
    def _():
        run_ref[...] = jnp.zeros_like(run_ref)

    x = x_ref[...]
    xh = x.astype(BF16)
    xl = (x - xh.astype(F32)).astype(BF16)
    wh = wh_ref[...]
    logits = _dot_nt(wh, xh) + _dot_nt(wh, xl) + _dot_nt(wl_ref[...], xh)
    scores = jax.nn.sigmoid(logits)
    sel = scores + bias_ref[...]
    tm = sel.shape[1]
    ninf = -jnp.inf

    gi = lax.broadcasted_iota(jnp.int32, (GROUP_SIZE, tm), 0).astype(F32)
    g_rows = []
    for g in range(N_GROUPS):
        seg = sel[g * GROUP_SIZE:(g + 1) * GROUP_SIZE, :]
        m1 = jnp.max(seg, axis=0, keepdims=True)
        first = _first_index(seg == m1, gi, float(GROUP_SIZE))
        m2 = jnp.max(jnp.where(gi == first, ninf, seg), axis=0, keepdims=True)
        g_rows.append(m1 + m2)
    g_score = jnp.concatenate(g_rows, axis=0)

    ggi = lax.broadcasted_iota(jnp.int32, (N_GROUPS, tm), 0).astype(F32)
    g_keep = jnp.zeros((N_GROUPS, tm), F32)
    cur = g_score
    for _ in range(TOPK_GROUPS):
        mx = jnp.max(cur, axis=0, keepdims=True)
        hit = ggi == _first_index(cur == mx, ggi, float(N_GROUPS))
        g_keep = jnp.where(hit, 1.0, g_keep)
        cur = jnp.where(hit, ninf, cur)

    keep = jnp.concatenate(
        [jnp.broadcast_to(g_keep[g:g + 1, :], (GROUP_SIZE, tm)) for g in range(N_GROUPS)], axis=0)
    cur = jnp.where(keep > 0.0, sel, NEG)
    ei = lax.broadcasted_iota(jnp.int32, (N_EXPERTS, tm), 0).astype(F32)
    idx_rows, w_rows = [], []
    chosen = jnp.zeros((N_EXPERTS, tm), F32)
    for _ in range(TOP_K):
        mx = jnp.max(cur, axis=0, keepdims=True)
        first = _first_index(cur == mx, ei, float(N_EXPERTS))
        hit = ei == first
        idx_rows.append(first)
        w_rows.append(jnp.sum(jnp.where(hit, scores, 0.0), axis=0, keepdims=True))
        cur = jnp.where(hit, ninf, cur)
        chosen = jnp.where(hit, 1.0, chosen)
    w = jnp.concatenate(w_rows, axis=0)
    idx_ref[...] = jnp.concatenate(idx_rows, axis=0).astype(jnp.int32)
    wts_ref[...] = w / jnp.sum(w, axis=0, keepdims=True) * ROUTED_SCALE

    t_src = lax.broadcasted_iota(jnp.int32, (tm, tm), 0)
    t_dst = lax.broadcasted_iota(jnp.int32, (tm, tm), 1)
    before = (t_src < t_dst).astype(BF16)
    rank = _dot(chosen.astype(BF16), before) + run_ref[...]
    pos_ref[...] = jnp.concatenate(
        [jnp.sum(jnp.where(ei == r, rank, 0.0), axis=0, keepdims=True) for r in idx_rows],
        axis=0).astype(jnp.int32)
    run = run_ref[...] + jnp.sum(chosen, axis=1, keepdims=True)
    run_ref[...] = run
    cnt_ref[...] = run.astype(jnp.int32)


def _router_call(x, w_router, router_bias):
    t, d = x.shape
    tm = min(512, t)
    wt = w_router.T
    wh = wt.astype(BF16)
    wl = (wt - wh.astype(F32)).astype(BF16)
    kt = lambda: pl.BlockSpec((TOP_K, tm), lambda i: (0, i))
    return pl.pallas_call(
        _router_kernel,
        grid=(t // tm,),
        in_specs=[pl.BlockSpec((tm, d), lambda i: (i, 0)),
                  pl.BlockSpec((N_EXPERTS, d), lambda i: (0, 0)),
                  pl.BlockSpec((N_EXPERTS, d), lambda i: (0, 0)),
                  pl.BlockSpec((N_EXPERTS, 1), lambda i: (0, 0))],
        out_specs=[kt(), kt(), kt(), pl.BlockSpec((N_EXPERTS, 1), lambda i: (0, 0))],
        out_shape=[jax.ShapeDtypeStruct((TOP_K, t), jnp.int32),
                   jax.ShapeDtypeStruct((TOP_K, t), F32),
                   jax.ShapeDtypeStruct((TOP_K, t), jnp.int32),
                   jax.ShapeDtypeStruct((N_EXPERTS, 1), jnp.int32)],
        scratch_shapes=[pltpu.VMEM((N_EXPERTS, 1), F32)],
        compiler_params=_cparams("arbitrary"),
        name="router",
    )(x, wh, wl, router_bias.astype(F32).reshape(N_EXPERTS, 1))


def _dest_kernel(idx_ref, pos_ref, start_ref, dest_ref):
    tm = idx_ref.shape[1]
    ei = lax.broadcasted_iota(jnp.int32, (N_EXPERTS, tm), 0)
    start = start_ref[...].astype(F32)
    rows = [jnp.sum(jnp.where(ei == idx_ref[k:k + 1, :], start, 0.0), axis=0, keepdims=True)
            for k in range(TOP_K)]
    dest_ref[...] = pos_ref[...] + jnp.concatenate(rows, axis=0).astype(jnp.int32)


def _dest_call(idx_kt, pos_kt, p_start):
    t = idx_kt.shape[1]
    tm = min(1024, t)
    kt = lambda: pl.BlockSpec((TOP_K, tm), lambda i: (0, i))
    return pl.pallas_call(
        _dest_kernel,
        grid=(t // tm,),
        in_specs=[kt(), kt(), pl.BlockSpec((N_EXPERTS, 1), lambda i: (0, 0))],
        out_specs=kt(),
        out_shape=jax.ShapeDtypeStruct((TOP_K, t), jnp.int32),
        compiler_params=_cparams("parallel"),
        name="moe_dest",
    )(idx_kt, pos_kt, p_start.reshape(N_EXPERTS, 1))


LANES = 128
ROW_SUB = D_MODEL // LANES
DISPATCH_TILE = 256


def _to_row_tiles(dst_ref, v):
    for j in range(ROW_SUB):
        dst_ref[:, j, :] = v[:, j * LANES:(j + 1) * LANES]


def _dispatch_kernel(valid_ref, dest_ref, x_ref, xs_hbm, xt, zbuf, sem, zsem):
    tm = xt.shape[0]
    bm = zbuf.shape[0]
    n_blk = valid_ref.shape[0]

    @pl.when(pl.program_id(0) == 0)
    def _():
        zbuf[...] = jnp.zeros_like(zbuf)

        def zero_copy(i):
            return pltpu.make_async_copy(zbuf, xs_hbm.at[pl.ds(i * bm, bm)], zsem)

        def start(i, carry):
            @pl.when(valid_ref[i] < bm)
            def _():
                zero_copy(i).start()
            return carry

        def wait(i, carry):
            @pl.when(valid_ref[i] < bm)
            def _():
                zero_copy(i).wait()
            return carry

        lax.fori_loop(0, n_blk, start, 0)
        lax.fori_loop(0, n_blk, wait, 0)

    _to_row_tiles(xt, x_ref[...])

    def body(r, carry):
        for k in range(TOP_K):
            pltpu.make_async_copy(xt.at[r], xs_hbm.at[dest_ref[k, r]], sem).start()
        return carry

    lax.fori_loop(0, tm, body, 0)
    for k in range(TOP_K):
        pltpu.make_async_copy(xt, xs_hbm.at[pl.ds(0, tm)], sem).wait()


def _dispatch_call(x, dest_kt, blk_valid):
    t, d = x.shape
    tm = min(DISPATCH_TILE, t)
    bm = MOE_BLOCK
    n_blk = blk_valid.shape[0]
    grid_spec = pltpu.PrefetchScalarGridSpec(
        num_scalar_prefetch=1,
        grid=(t // tm,),
        in_specs=[pl.BlockSpec((TOP_K, tm), lambda i, v: (0, i), memory_space=pltpu.SMEM),
                  pl.BlockSpec((tm, d), lambda i, v: (i, 0))],
        out_specs=pl.BlockSpec(memory_space=pl.ANY),
        scratch_shapes=[pltpu.VMEM((tm, ROW_SUB, LANES), F32), pltpu.VMEM((bm, ROW_SUB, LANES), F32),
                        pltpu.SemaphoreType.DMA(()), pltpu.SemaphoreType.DMA(())],
    )
    return pl.pallas_call(
        _dispatch_kernel,
        grid_spec=grid_spec,
        out_shape=jax.ShapeDtypeStruct((n_blk * bm, ROW_SUB, LANES), F32),
        compiler_params=_cparams("arbitrary"),
        name="moe_dispatch",
    )(blk_valid, dest_kt, x)


def _expert_kernel(blk_e_ref, nused_ref, xs_ref, w13_ref, w2_ref, o_ref):
    i = pl.program_id(0)

    @pl.when(i < nused_ref[0])
    def _():
        xb = jnp.concatenate([xs_ref[:, j, :] for j in range(ROW_SUB)], axis=1).astype(BF16)
        h13 = _dot(xb, w13_ref[0])
        hid = _silu(h13[:, :D_EXPERT]) * h13[:, D_EXPERT:]
        _to_row_tiles(o_ref, _dot(hid.astype(BF16), w2_ref[0]))

    @pl.when(i >= nused_ref[0])
    def _():
        o_ref[...] = jnp.zeros_like(o_ref)


def _expert_call(xs, blk_e, n_used, w13, w2):
    bm = MOE_BLOCK
    n_blk = blk_e.shape[0]
    d = D_MODEL
    xs_idx = lambda i, e, n: (jnp.minimum(i, jnp.maximum(n[0] - 1, 0)), 0, 0)
    grid_spec = pltpu.PrefetchScalarGridSpec(
        num_scalar_prefetch=2,
        grid=(n_blk,),
        in_specs=[pl.BlockSpec((bm, ROW_SUB, LANES), xs_idx),
                  pl.BlockSpec((1, d, 2 * D_EXPERT), lambda i, e, n: (e[i], 0, 0)),
                  pl.BlockSpec((1, D_EXPERT, d), lambda i, e, n: (e[i], 0, 0))],
        out_specs=pl.BlockSpec((bm, ROW_SUB, LANES), lambda i, e, n: (i, 0, 0)),
    )
    return pl.pallas_call(
        _expert_kernel,
        grid_spec=grid_spec,
        out_shape=jax.ShapeDtypeStruct((n_blk * bm, ROW_SUB, LANES), F32),
        compiler_params=_cparams("arbitrary"),
        name="routed_experts",
    )(blk_e, n_used, xs, w13, w2)


def _combine_kernel(dest_ref, destn_ref, ys_hbm, w_ref, dense_ref, x_ref, g_ref, b_ref,
                    o_ref, ob_ref, gbuf, sem, *, alpha):
    i = pl.program_id(0)
    n = pl.num_programs(0)
    rows = gbuf.shape[1]
    tm = rows // TOP_K
    cur = i % 2

    def issue(d_ref, s):
        def body(r, carry):
            for k in range(TOP_K):
                pltpu.make_async_copy(ys_hbm.at[d_ref[k, r]], gbuf.at[s, k * tm + r], sem.at[s]).start()
            return carry
        lax.fori_loop(0, tm, body, 0)

    @pl.when(i == 0)
    def _():
        issue(dest_ref, 0)

    @pl.when(i + 1 < n)
    def _():
        issue(destn_ref, 1 - cur)

    pltpu.make_async_copy(ys_hbm.at[pl.ds(0, rows)], gbuf.at[cur], sem.at[cur]).wait()
    w = w_ref[...]
    chunks = []
    for j in range(ROW_SUB):
        acc = gbuf[cur, 0:tm, j, :] * w[:, 0:1]
        for k in range(1, TOP_K):
            acc = acc + gbuf[cur, k * tm:(k + 1) * tm, j, :] * w[:, k:k + 1]
        chunks.append(acc)
    routed = jnp.concatenate(chunks, axis=1)
    y = _layer_norm_rows(alpha * x_ref[...] + routed + dense_ref[...], g_ref[...], b_ref[...])
    o_ref[...] = y
    ob_ref[...] = y.astype(BF16)


def _combine_call(ys, dest_kt, wts, dense, x, g, b, alpha):
    t, d = x.shape
    tm = min(COMBINE_TILE, t)
    nt = t // tm
    rows = tm * TOP_K
    tile = lambda: pl.BlockSpec((tm, d), lambda i: (i, 0))
    vec = lambda: pl.BlockSpec((1, d), lambda i: (0, 0))
    return pl.pallas_call(
        functools.partial(_combine_kernel, alpha=alpha),
        grid=(nt,),
        in_specs=[pl.BlockSpec((TOP_K, tm), lambda i: (0, i), memory_space=pltpu.SMEM),
                  pl.BlockSpec((TOP_K, tm), lambda i: (0, jnp.minimum(i + 1, nt - 1)),
                               memory_space=pltpu.SMEM),
                  pl.BlockSpec(memory_space=pl.ANY),
                  pl.BlockSpec((tm, TOP_K), lambda i: (i, 0)),
                  tile(), tile(), vec(), vec()],
        out_specs=[tile(), tile()],
        out_shape=[jax.ShapeDtypeStruct((t, d), F32), jax.ShapeDtypeStruct((t, d), BF16)],
        scratch_shapes=[pltpu.VMEM((2, rows, ROW_SUB, LANES), F32), pltpu.SemaphoreType.DMA((2,))],
        compiler_params=_cparams("arbitrary"),
        name="combine_ln2",
    )(dest_kt, dest_kt, ys, wts, dense, x, g.reshape(1, d), b.reshape(1, d))


def _block_tables(counts, n_blk):
    bm = MOE_BLOCK
    padded = (counts + bm - 1) // bm * bm
    p_end = jnp.cumsum(padded)
    p_start = p_end - padded
    blk_row = jnp.arange(n_blk, dtype=jnp.int32) * bm
    blk_e = jnp.minimum(jnp.searchsorted(p_end, blk_row, side='right'), N_EXPERTS - 1).astype(jnp.int32)
    blk_valid = jnp.clip(counts[blk_e] - (blk_row - p_start[blk_e]), 0, bm).astype(jnp.int32)
    n_used = (p_end[-1] // bm).astype(jnp.int32).reshape(1)
    return p_start.astype(jnp.int32), blk_e, blk_valid, n_used


def _prep_layer(li, lower, w_in, b_in, conv_w, conv_b, norm_a, norm_b, w_out_a, w_out_b, w_o,
                ln1_g, ln1_b, w_ple, w_ple_gate, w_router, router_bias, w1, w3, w2, ws1, ws3, ws2,
                ln2_g, ln2_b):
    d = D_MODEL
    w = w_in[li]
    bi = b_in[li]
    seg = lambda a, lo, hi: a[..., lo:hi]
    o_qkv, o_oa, o_g = 0, 3 * d, 4 * d
    o_qb = 4 * d + 4 * H_A
    o_ff, o_ib, o_gb, o_ga = o_qb + d, o_qb + 3 * d, o_qb + 4 * d, o_qb + 5 * d
    cat = lambda parts: jnp.concatenate(parts, axis=-1)
    raw_cols = [(o_qkv, o_qkv + 3 * d), (o_ib, o_ib + d)]
    sig_cols = [(o_oa, o_oa + d), (o_ga, o_ga + 2 * d)]
    silu_cols = [(o_qb, o_qb + d), (o_gb, o_gb + d)]
    return dict(
        w_raw=cat([seg(w, *c) for c in raw_cols]).astype(BF16), b_raw=cat([seg(bi, *c) for c in raw_cols]),
        w_sig=cat([seg(w, *c) for c in sig_cols]).astype(BF16), b_sig=cat([seg(bi, *c) for c in sig_cols]),
        w_silu=cat([seg(w, *c) for c in silu_cols]).astype(BF16), b_silu=cat([seg(bi, *c) for c in silu_cols]),
        w_fg=seg(w, o_ff, o_ff + 2 * d).astype(BF16), b_fg=seg(bi, o_ff, o_ff + 2 * d),
        w_g=seg(w, o_g, o_g + 4 * H_A).astype(BF16), b_g=seg(bi, o_g, o_g + 4 * H_A),
        lower=lower[li].reshape(-1),
        conv_w=conv_w[li], conv_b=conv_b[li], norm_a=norm_a[li], norm_b=norm_b[li],
        w_out_a=w_out_a[li].astype(BF16), w_out_b=w_out_b[li].astype(BF16), w_o=w_o[li].astype(BF16),
        ln1_g=ln1_g[li], ln1_b=ln1_b[li],
        w_ple=w_ple[li].astype(BF16), w_ple_gate=w_ple_gate[li].astype(BF16),
        w_router=w_router[li], router_bias=router_bias[li],
        w13=jnp.concatenate([w1[li], w3[li]], axis=-1).astype(BF16), w2=w2[li].astype(BF16),
        ws13=jnp.concatenate([ws1[li], ws3[li]], axis=-1).astype(BF16), ws2=ws2[li].astype(BF16),
        ln2_g=ln2_g[li], ln2_b=ln2_b[li],
    )


def _layer(x, xb, p, lp, bsz, s, alpha):
    t, d = x.shape
    a = _proj_act(xb, lp["w_raw"], lp["b_raw"], "raw", "proj_raw")
    sg = _proj_act(xb, lp["w_sig"], lp["b_sig"], "sigmoid", "proj_sigmoid")
    sl = _proj_act(xb, lp["w_silu"], lp["b_silu"], "silu", "proj_silu")
    kf, lf = _proj_forget(xb, lp["w_fg"], lp["b_fg"], lp["lower"])
    gc, gr = _proj_gates(xb, lp["w_g"], lp["b_g"])

    a3 = a.reshape(bsz, s, 4 * d)
    qk3 = _conv_silu(a3, lp["conv_w"], lp["conv_b"])
    gc3 = gc.reshape(bsz, s, 4 * H_A)
    haf = _mlstm_dir(qk3, a3, gc3, gr, False).reshape(t, d)
    hab = _mlstm_dir(qk3, a3, gc3, gr, True).reshape(t, d)

    sl3 = sl.reshape(bsz, s, 2 * d)
    kf3 = kf.reshape(bsz, s, 2 * d)
    lf3 = lf.reshape(bsz, s, 2 * d)
    hbf = _hgrn_dir(sl3, kf3, a3, lf3, False).reshape(t, d)
    hbb = _hgrn_dir(sl3, kf3, a3, lf3, True).reshape(t, d)

    x1, x1b = _mix_call(x, haf, hab, hbf, hbb, sg, sl, lp["norm_a"], lp["norm_b"],
                        lp["w_out_a"], lp["w_out_b"], lp["w_o"], lp["ln1_g"], lp["ln1_b"], alpha)

    dense = _dense_call(x1b, p, lp["ws13"], lp["ws2"], lp["w_ple"], lp["w_ple_gate"])
    idx_kt, wts_kt, pos_kt, counts = _router_call(x1, lp["w_router"], lp["router_bias"])
    n_blk = t * TOP_K // MOE_BLOCK + N_EXPERTS
    p_start, blk_e, blk_valid, n_used = _block_tables(counts.reshape(-1), n_blk)
    dest_kt = _dest_call(idx_kt, pos_kt, p_start)
    xs = _dispatch_call(x1, dest_kt, blk_valid)
    ys = _expert_call(xs, blk_e, n_used, lp["w13"], lp["w2"])
    return _combine_call(ys, dest_kt, wts_kt.T, dense, x1, lp["ln2_g"], lp["ln2_b"], alpha)


def kernel(x_prompt, x_sample, p_prompt, p_sample, emb_ln_g, emb_ln_b, w_in, b_in, conv_w, conv_b, norm_a, norm_b, lb_param, w_out_a, w_out_b, w_o, ln1_g, ln1_b, w_ple, w_ple_gate, w_router, router_bias, w1, w3, w2, ws1, ws3, ws2, ln2_g, ln2_b):
    depth = w_in.shape[0]
    alpha = (2.0 * depth) ** 0.25
    lb_soft = jax.nn.softmax(lb_param.astype(F32), axis=0)
    lower = jnp.cumsum(lb_soft, axis=0) - lb_soft[0:1]
    layers = [_prep_layer(li, lower, w_in, b_in, conv_w, conv_b, norm_a, norm_b, w_out_a, w_out_b, w_o,
                          ln1_g, ln1_b, w_ple, w_ple_gate, w_router, router_bias, w1, w3, w2,
                          ws1, ws3, ws2, ln2_g, ln2_b) for li in range(depth)]

    def trunk(x3, p4):
        bsz, s, d = x3.shape
        x, xb = _layer_norm_call(x3.reshape(bsz * s, d), emb_ln_g, emb_ln_b)
        for li in range(depth):
            x, xb = _layer(x, xb, p4[li].reshape(bsz * s, -1), layers[li], bsz, s, alpha)
        return x.reshape(bsz, s, d)

    return (trunk(x_prompt, p_prompt), trunk(x_sample, p_sample))
```

```python
import functools

import jax
import jax.numpy as jnp
from jax import lax
from jax.experimental import pallas as pl
from jax.experimental.pallas import tpu as pltpu

F32 = jnp.float32
BF16 = jnp.bfloat16

D_MODEL = 1024
PLE_DIM = 256
H_A = 4
DH_A = D_MODEL // H_A
CHUNK_A = 128
H_B = 8
DK_B = 128
CHUNK_B = 128
N_EXPERTS = 256
TOP_K = 8
N_GROUPS = 8
GROUP_SIZE = N_EXPERTS // N_GROUPS
TOPK_GROUPS = 4
D_EXPERT = 256
ROUTED_SCALE = 2.5
EPS = 1e-5
NEG = -1e30
TINY = 1e-30

HGRN_SAFE_RANGE = 80.0
HGRN_ROWS = CHUNK_B
MOE_BLOCK = 512
COMBINE_TILE = 128
VMEM_LIMIT = 56 * 1024 * 1024


def _cparams(*sem):
    return pltpu.CompilerParams(dimension_semantics=sem, vmem_limit_bytes=VMEM_LIMIT)


def _split3(x):
    hi = x.astype(BF16)
    r1 = x - hi.astype(F32)
    mid = r1.astype(BF16)
    lo = (r1 - mid.astype(F32)).astype(BF16)
    return hi, mid, lo


def _dot(a, b):
    return jnp.dot(a, b, preferred_element_type=F32)


def _dot_nt(a, b):
    return lax.dot_general(a, b, (((1,), (1,)), ((), ())), preferred_element_type=F32)


def _dot_tn(a, b):
    return lax.dot_general(a, b, (((0,), (0,)), ((), ())), preferred_element_type=F32)


def _layer_norm_rows(v, g, b):
    mu = jnp.mean(v, axis=-1, keepdims=True)
    vc = v - mu
    var = jnp.mean(vc * vc, axis=-1, keepdims=True)
    return vc * lax.rsqrt(var + EPS) * g + b


def _log_sigmoid(z):
    return jnp.minimum(z, 0.0) - jnp.log1p(jnp.exp(-jnp.abs(z)))


def _silu(z):
    return z * jax.nn.sigmoid(z)


def _ln_kernel(x_ref, g_ref, b_ref, o_ref, ob_ref):
    y = _layer_norm_rows(x_ref[...], g_ref[...], b_ref[...])
    o_ref[...] = y
    ob_ref[...] = y.astype(BF16)


def _layer_norm_call(x, g, b):
    t, d = x.shape
    tm = min(512, t)
    return pl.pallas_call(
        _ln_kernel,
        grid=(t // tm,),
        in_specs=[pl.BlockSpec((tm, d), lambda i: (i, 0)),
                  pl.BlockSpec((1, d), lambda i: (0, 0)),
                  pl.BlockSpec((1, d), lambda i: (0, 0))],
        out_specs=[pl.BlockSpec((tm, d), lambda i: (i, 0)),
                   pl.BlockSpec((tm, d), lambda i: (i, 0))],
        out_shape=[jax.ShapeDtypeStruct((t, d), F32), jax.ShapeDtypeStruct((t, d), BF16)],
        compiler_params=_cparams("parallel"),
        name="emb_layer_norm",
    )(x, g.reshape(1, d), b.reshape(1, d))


PROJ_SUB = 256


def _row_blocks(n):
    step = min(PROJ_SUB, n)
    return [slice(r, r + step) for r in range(0, n, step)]


def _proj_act_kernel(x_ref, w_ref, b_ref, o_ref, *, act):
    for rs in _row_blocks(x_ref.shape[0]):
        z = _dot(x_ref[rs, :], w_ref[...]) + b_ref[...]
        if act == "sigmoid":
            z = jax.nn.sigmoid(z)
        elif act == "silu":
            z = _silu(z)
        o_ref[rs, :] = z.astype(o_ref.dtype)


def _proj_act(xb, w, b, act, name):
    t, d = xb.shape
    n = w.shape[1]
    tm = min(1024, t)
    tn = 1024
    return pl.pallas_call(
        functools.partial(_proj_act_kernel, act=act),
        grid=(t // tm, n // tn),
        in_specs=[pl.BlockSpec((tm, d), lambda i, j: (i, 0)),
                  pl.BlockSpec((d, tn), lambda i, j: (0, j)),
                  pl.BlockSpec((1, tn), lambda i, j: (0, j))],
        out_specs=pl.BlockSpec((tm, tn), lambda i, j: (i, j)),
        out_shape=jax.ShapeDtypeStruct((t, n), BF16),
        compiler_params=_cparams("parallel", "arbitrary"),
        name=name,
    )(xb, w, b.reshape(1, n))


def _proj_forget_kernel(x_ref, w_ref, b_ref, oml_ref, la_ref, lc_ref, k_ref, lf_ref):
    a = la_ref[...]
    for rs in _row_blocks(x_ref.shape[0]):
        z = _dot(x_ref[rs, :], w_ref[...]) + b_ref[...]
        e = jnp.exp(-jnp.abs(z))
        r = 1.0 / (1.0 + e)
        k_ref[rs, :] = (oml_ref[...] * (jnp.where(z > 0.0, e, 1.0) * r)).astype(BF16)
        u = lc_ref[...] + (jnp.minimum(z, 0.0) + jnp.log(r))
        lf_ref[rs, :] = jnp.maximum(a, u) + jnp.log(1.0 + jnp.exp(-jnp.abs(a - u)))


def _proj_forget(xb, w, b, lower):
    t, d = xb.shape
    n = w.shape[1]
    tm = min(1024, t)
    tn = 1024
    lb = lower.reshape(1, n)
    vec = lambda: pl.BlockSpec((1, tn), lambda i, j: (0, j))
    return pl.pallas_call(
        _proj_forget_kernel,
        grid=(t // tm, n // tn),
        in_specs=[pl.BlockSpec((tm, d), lambda i, j: (i, 0)),
                  pl.BlockSpec((d, tn), lambda i, j: (0, j)),
                  vec(), vec(), vec(), vec()],
        out_specs=[pl.BlockSpec((tm, tn), lambda i, j: (i, j)),
                   pl.BlockSpec((tm, tn), lambda i, j: (i, j))],
        out_shape=[jax.ShapeDtypeStruct((t, n), BF16), jax.ShapeDtypeStruct((t, n), F32)],
        compiler_params=_cparams("parallel", "arbitrary"),
        name="proj_forget",
    )(xb, w, b.reshape(1, n), 1.0 - lb, jnp.log(jnp.maximum(lb, TINY)), jnp.log1p(-lb))


def _proj_gates_kernel(x_ref, w_ref, wt_ref, b_ref, bt_ref, gc_ref, gr_ref):
    x = x_ref[...]
    zc = _dot(x, w_ref[...]) + b_ref[...]
    lane = lax.broadcasted_iota(jnp.int32, zc.shape, 1)
    gc_ref[...] = jnp.where(lane >= 2 * H_A, _log_sigmoid(zc), zc)
    zr = _dot_nt(wt_ref[...], x) + bt_ref[...]
    sub = lax.broadcasted_iota(jnp.int32, zr.shape, 0)
    gr_ref[...] = jnp.where(sub >= 2 * H_A, _log_sigmoid(zr), zr)


def _proj_gates(xb, w, b):
    t, d = xb.shape
    n = w.shape[1]
    tm = min(1024, t)
    return pl.pallas_call(
        _proj_gates_kernel,
        grid=(t // tm,),
        in_specs=[pl.BlockSpec((tm, d), lambda i: (i, 0)),
                  pl.BlockSpec((d, n), lambda i: (0, 0)),
                  pl.BlockSpec((n, d), lambda i: (0, 0)),
                  pl.BlockSpec((1, n), lambda i: (0, 0)),
                  pl.BlockSpec((n, 1), lambda i: (0, 0))],
        out_specs=[pl.BlockSpec((tm, n), lambda i: (i, 0)),
                   pl.BlockSpec((n, tm), lambda i: (0, i))],
        out_shape=[jax.ShapeDtypeStruct((t, n), F32), jax.ShapeDtypeStruct((n, t), F32)],
        compiler_params=_cparams("parallel"),
        name="proj_gates",
    )(xb, w, w.T, b.reshape(1, n), b.reshape(n, 1))


def _conv_kernel(x_ref, w_ref, b_ref, o_ref, *, rows):
    s = x_ref.shape[1]
    n_rc = s // rows
    w = w_ref[...]
    w0, w1, w2 = w[0:1, :], w[1:2, :], w[2:3, :]
    bias = b_ref[...]
    ridx = lax.broadcasted_iota(jnp.int32, (rows, x_ref.shape[2]), 0)

    def body(r, carry):
        base = pl.multiple_of(r * rows, rows)
        cur = x_ref[0, pl.ds(base, rows), :].astype(F32)
        pstart = pl.multiple_of(jnp.maximum(base - 16, 0), 16)
        nstart = pl.multiple_of(jnp.minimum(base + rows, s - 16), 16)
        prev_row = x_ref[0, pl.ds(pstart, 16), :][15:16, :].astype(F32)
        next_row = x_ref[0, pl.ds(nstart, 16), :][0:1, :].astype(F32)
        prev_row = jnp.where(r > 0, prev_row, 0.0)
        next_row = jnp.where(r < n_rc - 1, next_row, 0.0)
        up = jnp.where(ridx == 0, prev_row, pltpu.roll(cur, 1, 0))
        dn = jnp.where(ridx == rows - 1, next_row, pltpu.roll(cur, rows - 1, 0))
        z = w0 * up + w1 * cur + w2 * dn + bias
        o_ref[0, pl.ds(base, rows), :] = _silu(z).astype(o_ref.dtype)
        return carry

    lax.fori_loop(0, n_rc, body, 0)


def _conv_silu(a3, conv_w, conv_b):
    bsz, s, _ = a3.shape
    c = conv_w.shape[1]
    cw = 256
    rows = min(256, s)
    return pl.pallas_call(
        functools.partial(_conv_kernel, rows=rows),
        grid=(bsz, c // cw),
        in_specs=[pl.BlockSpec((1, s, cw), lambda b, j: (b, 0, j)),
                  pl.BlockSpec((3, cw), lambda b, j: (0, j)),
                  pl.BlockSpec((1, cw), lambda b, j: (0, j))],
        out_specs=pl.BlockSpec((1, s, cw), lambda b, j: (b, 0, j)),
        out_shape=jax.ShapeDtypeStruct((bsz, s, c), BF16),
        compiler_params=_cparams("parallel", "parallel"),
        name="conv_silu",
    )(a3, conv_w, conv_b.reshape(1, c))


def _mlstm_kernel(q_ref, k_ref, v_ref, gc_ref, gr_ref, o_ref, c_ref, n_ref, m_ref, *, reverse):
    L = CHUNK_A

    @pl.when(pl.program_id(1) == 0)
    def _():
        c_ref[...] = jnp.zeros_like(c_ref)
        n_ref[...] = jnp.zeros_like(n_ref)
        m_ref[...] = jnp.zeros_like(m_ref)

    row = lax.broadcasted_iota(jnp.int32, (L, L), 0)
    col = lax.broadcasted_iota(jnp.int32, (L, L), 1)
    if reverse:
        mask = col >= row
        mask_t = row >= col
        last = 0
    else:
        mask = col <= row
        mask_t = row <= col
        last = L - 1
    m_mat = mask.astype(BF16)
    mt_mat = mask_t.astype(BF16)

    gc = gc_ref[0]
    gr = gr_ref[...]
    cum_c = sum(_dot(m_mat, p) for p in _split3(gc))
    cum_r = sum(_dot(p, mt_mat) for p in _split3(gr))
    off = H_A if reverse else 0
    scale = DH_A ** -0.5

    for h in range(H_A):
        ii = off + h
        fi = 2 * H_A + off + h
        hs = slice(h * DH_A, (h + 1) * DH_A)
        bc = cum_c[:, fi:fi + 1]
        br = cum_r[fi:fi + 1, :]
        ic = gc[:, ii:ii + 1]
        ir = gr[ii:ii + 1, :]
        m_st = m_ref[h:h + 1, 0:1]
        qh = q_ref[0, :, hs]
        kh = k_ref[0, :, hs]
        vh = v_ref[0, :, hs]
        c_st = c_ref[h]
        n_st = n_ref[h:h + 1, :]

        d_mat = jnp.where(mask, bc - br + ir, NEG)
        m_inter = bc + m_st
        m_row = jnp.maximum(m_inter, jnp.max(d_mat, axis=1, keepdims=True))
        w_inter = jnp.exp(m_inter - m_row)
        s_mat = _dot_nt(qh, kh) * scale * jnp.exp(d_mat - m_row)
        num = w_inter * _dot(qh, c_st.astype(BF16)) + _dot(s_mat.astype(BF16), vh)
        qn = jnp.sum(qh.astype(F32) * n_st, axis=1, keepdims=True)
        den = w_inter * qn + jnp.sum(s_mat, axis=1, keepdims=True)
        inv = 1.0 / jnp.maximum(jnp.abs(den), jnp.exp(-m_row))
        o_ref[0, :, hs] = num * inv

        b_last = bc[last:last + 1, :]
        g_end_r = b_last - br + ir
        g_end_c = b_last - bc + ic
        m_next = jnp.maximum(b_last + m_st, jnp.max(g_end_r, axis=1, keepdims=True))
        decay = jnp.exp(b_last + m_st - m_next)
        wk = jnp.exp(g_end_c - m_next) * (kh.astype(F32) * scale)
        c_ref[h] = decay * c_st + _dot_tn(wk.astype(BF16), vh)
        n_ref[h:h + 1, :] = decay * n_st + jnp.sum(wk, axis=0, keepdims=True)
        m_ref[h:h + 1, :] = jnp.broadcast_to(m_next, (1, m_ref.shape[1]))


def _mlstm_dir(qk3, a3, gc3, gr, reverse):
    bsz, s, _ = qk3.shape
    L = CHUNK_A
    nc = s // L
    cidx = (lambda c: nc - 1 - c) if reverse else (lambda c: c)
    return pl.pallas_call(
        functools.partial(_mlstm_kernel, reverse=reverse),
        grid=(bsz, nc),
        in_specs=[pl.BlockSpec((1, L, D_MODEL), lambda b, c: (b, cidx(c), 0)),
                  pl.BlockSpec((1, L, D_MODEL), lambda b, c: (b, cidx(c), 1)),
                  pl.BlockSpec((1, L, D_MODEL), lambda b, c: (b, cidx(c), 2)),
                  pl.BlockSpec((1, L, 4 * H_A), lambda b, c: (b, cidx(c), 0)),
                  pl.BlockSpec((4 * H_A, L), lambda b, c: (0, b * nc + cidx(c)))],
        out_specs=pl.BlockSpec((1, L, D_MODEL), lambda b, c: (b, cidx(c), 0)),
        out_shape=jax.ShapeDtypeStruct((bsz, s, D_MODEL), F32),
        scratch_shapes=[pltpu.VMEM((H_A, DH_A, DH_A), F32),
                        pltpu.VMEM((H_A, DH_A), F32),
                        pltpu.VMEM((H_A, 128), F32)],
        compiler_params=_cparams("parallel", "arbitrary"),
        name="mlstm_bwd" if reverse else "mlstm_fwd",
    )(qk3, qk3, a3, gc3, gr)


def _hgrn_kernel(q_ref, k_ref, v_ref, lf_ref, o_ref, st_ref, oi_ref, b_ref, q32_ref, k32_ref, v32_ref, *,
                 reverse):
    L = CHUNK_B
    rows_blk = q_ref.shape[1]
    n_sub = rows_blk // L
    width = q_ref.shape[2]

    @pl.when(pl.program_id(1) == 0)
    def _():
        st_ref[...] = jnp.zeros_like(st_ref)

    row = lax.broadcasted_iota(jnp.int32, (rows_blk, rows_blk), 0)
    col = lax.broadcasted_iota(jnp.int32, (rows_blk, rows_blk), 1)
    vis = (col >= row) if reverse else (col <= row)
    same = (row // L) == (col // L)
    last = 0 if reverse else L - 1
    mask = vis[0:L, 0:L]

    m_all = jnp.logical_and(vis, same).astype(BF16)
    b = sum(_dot(m_all, p) for p in _split3(lf_ref[0]))

    def per_chunk_row(r):
        return jnp.concatenate(
            [jnp.broadcast_to(b[s * L + r:s * L + r + 1, :], (L, width)) for s in range(n_sub)], axis=0)

    q = q_ref[0].astype(F32)
    k = k_ref[0].astype(F32)
    v = v_ref[0]
    q_in = (q * jnp.exp(b)).astype(BF16)
    k_out = (k * jnp.exp(per_chunk_row(last) - b)).astype(BF16)
    dev = b - per_chunk_row(L // 2)
    rng = jnp.max(jnp.abs(dev))
    qt = (q * jnp.exp(dev)).astype(BF16)
    kt = (k * jnp.exp(-dev)).astype(BF16)

    for s in (reversed(range(n_sub)) if reverse else range(n_sub)):
        rs = slice(s * L, (s + 1) * L)
        dec = jnp.exp(b[s * L + last:s * L + last + 1, :])
        for h in range(H_B):
            hs = slice(h * DK_B, (h + 1) * DK_B)
            st = st_ref[h]
            o_inter = _dot_nt(q_in[rs, hs], st.astype(BF16))
            a = jnp.where(mask, _dot_nt(qt[rs, hs], kt[rs, hs]), 0.0)
            o_ref[0, rs, hs] = o_inter + _dot(a.astype(BF16), v[rs, hs])
            oi_ref[rs, hs] = o_inter
            st_ref[h] = dec[:, hs] * st + _dot_tn(v[rs, hs], k_out[rs, hs])

    @pl.when(rng > HGRN_SAFE_RANGE)
    def _():
        b_ref[...] = b
        q32_ref[...] = q
        k32_ref[...] = k
        v32_ref[...] = v.astype(F32)
        o_ref[0] = oi_ref[...]
        sidx = lax.broadcasted_iota(jnp.int32, (rows_blk, 1), 0)

        def body(j, carry):
            is_j = sidx == j
            b_all = b_ref[...]
            bj = jnp.sum(jnp.where(is_j, b_all, 0.0), axis=0, keepdims=True)
            qj = jnp.sum(jnp.where(is_j, q32_ref[...], 0.0), axis=0, keepdims=True)
            valid = jnp.logical_and((sidx >= j) if reverse else (sidx <= j), sidx // L == j // L)
            e = jnp.exp(jnp.minimum(bj - b_all, 0.0))
            t = jnp.where(valid, k32_ref[...] * e * qj, 0.0)
            for h in range(H_B):
                hs = slice(h * DK_B, (h + 1) * DK_B)
                a = jnp.sum(t[:, hs], axis=1, keepdims=True)
                row_j = jnp.sum(a * v32_ref[:, hs], axis=0, keepdims=True)
                o_ref[0, :, hs] += jnp.where(is_j, row_j, 0.0)
            return carry

        lax.fori_loop(0, rows_blk, body, 0)


def _hgrn_dir(qs3, kf3, a3, lf3, reverse):
    bsz, s, _ = qs3.shape
    L = HGRN_ROWS
    nc = s // L
    w = D_MODEL
    cidx = (lambda c: nc - 1 - c) if reverse else (lambda c: c)
    d = 1 if reverse else 0
    return pl.pallas_call(
        functools.partial(_hgrn_kernel, reverse=reverse),
        grid=(bsz, nc),
        in_specs=[pl.BlockSpec((1, L, w), lambda b, c: (b, cidx(c), 0)),
                  pl.BlockSpec((1, L, w), lambda b, c: (b, cidx(c), d)),
                  pl.BlockSpec((1, L, w), lambda b, c: (b, cidx(c), 3)),
                  pl.BlockSpec((1, L, w), lambda b, c: (b, cidx(c), d))],
        out_specs=pl.BlockSpec((1, L, w), lambda b, c: (b, cidx(c), 0)),
        out_shape=jax.ShapeDtypeStruct((bsz, s, w), F32),
        scratch_shapes=[pltpu.VMEM((H_B, DK_B, DK_B), F32)] + [pltpu.VMEM((L, w), F32)] * 5,
        compiler_params=_cparams("parallel", "arbitrary"),
        name="hgrn_bwd" if reverse else "hgrn_fwd",
    )(qs3, kf3, a3, lf3)


def _head_norm(v, n_heads, center):
    width = v.shape[1] // n_heads
    parts = []
    for h in range(n_heads):
        seg = v[:, h * width:(h + 1) * width]
        if center:
            seg = seg - jnp.mean(seg, axis=1, keepdims=True)
        parts.append(seg * lax.rsqrt(jnp.mean(seg * seg, axis=1, keepdims=True) + EPS))
    return jnp.concatenate(parts, axis=1)


def _mix_kernel(x_ref, haf_ref, hab_ref, hbf_ref, hbb_ref, og_ref, ga_ref, gb_ref, sgb_ref,
                na_ref, nb_ref, woa_ref, wob_ref, wo_ref, g_ref, b_ref, o_ref, ob_ref, *, alpha):
    h_a = og_ref[...].astype(F32) * (_head_norm(haf_ref[...] + hab_ref[...], H_A, True) * na_ref[...])
    y_a = _dot(h_a.astype(BF16), woa_ref[...])
    h_b = (_head_norm(hbf_ref[...] + hbb_ref[...], H_B, False) * nb_ref[...]) * sgb_ref[...].astype(F32)
    y_b = _dot(h_b.astype(BF16), wob_ref[...])
    merged = ga_ref[...].astype(F32) * y_a + gb_ref[...].astype(F32) * y_b
    mix = _dot(merged.astype(BF16), wo_ref[...])
    y = _layer_norm_rows(alpha * x_ref[...] + mix, g_ref[...], b_ref[...])
    o_ref[...] = y
    ob_ref[...] = y.astype(BF16)


def _mix_call(x, haf, hab, hbf, hbb, sg, sl, norm_a, norm_b, woa, wob, wo, g, b, alpha):
    t, d = x.shape
    tm = min(256, t)
    tile = lambda j=0: pl.BlockSpec((tm, d), lambda i: (i, j))
    vec = lambda: pl.BlockSpec((1, d), lambda i: (0, 0))
    mat = lambda: pl.BlockSpec((d, d), lambda i: (0, 0))
    r = lambda a: a.reshape(1, d)
    return pl.pallas_call(
        functools.partial(_mix_kernel, alpha=alpha),
        grid=(t // tm,),
        in_specs=[tile(), tile(), tile(), tile(), tile(),
                  tile(0), tile(1), tile(2), tile(1),
                  vec(), vec(), mat(), mat(), mat(), vec(), vec()],
        out_specs=[tile(), tile()],
        out_shape=[jax.ShapeDtypeStruct((t, d), F32), jax.ShapeDtypeStruct((t, d), BF16)],
        compiler_params=_cparams("parallel"),
        name="mix_out_ln1",
    )(x, haf, hab, hbf, hbb, sg, sg, sg, sl, r(norm_a), r(norm_b), woa, wob, wo, r(g), r(b))


def _dense_kernel(xb_ref, p_ref, ws13_ref, ws2_ref, wple_ref, wpg_ref, o_ref):
    xb = xb_ref[...]
    h13 = _dot(xb, ws13_ref[...])
    hid = _silu(h13[:, :D_EXPERT]) * h13[:, D_EXPERT:]
    shared = _dot(hid.astype(BF16), ws2_ref[...])
    ple = _dot(p_ref[...].astype(BF16), wple_ref[...]) * jax.nn.sigmoid(_dot(xb, wpg_ref[...]))
    o_ref[...] = shared + ple


def _dense_call(xb, p, ws13, ws2, wple, wpg):
    t, d = xb.shape
    tm = min(512, t)
    full = lambda a: pl.BlockSpec(a.shape, lambda i: (0, 0))
    return pl.pallas_call(
        _dense_kernel,
        grid=(t // tm,),
        in_specs=[pl.BlockSpec((tm, d), lambda i: (i, 0)),
                  pl.BlockSpec((tm, PLE_DIM), lambda i: (i, 0)),
                  full(ws13), full(ws2), full(wple), full(wpg)],
        out_specs=pl.BlockSpec((tm, d), lambda i: (i, 0)),
        out_shape=jax.ShapeDtypeStruct((t, d), F32),
        compiler_params=_cparams("parallel"),
        name="shared_ple",
    )(xb, p, ws13, ws2, wple, wpg)


def _first_index(hit, iota, big):
    return jnp.min(jnp.where(hit, iota, big), axis=0, keepdims=True)


def _router_kernel(x_ref, wh_ref, wl_ref, bias_ref, idx_ref, wts_ref, pos_ref, cnt_ref, run_ref):
    @pl.when(pl.program_id(0) == 0)
    def _():
        run_ref[...] = jnp.zeros_like(run_ref)

    x = x_ref[...]
    xh = x.astype(BF16)
    xl = (x - xh.astype(F32)).astype(BF16)
    wh = wh_ref[...]
    logits = _dot_nt(wh, xh) + _dot_nt(wh, xl) + _dot_nt(wl_ref[...], xh)
    scores = jax.nn.sigmoid(logits)
    sel = scores + bias_ref[...]
    tm = sel.shape[1]
    ninf = -jnp.inf

    gi = lax.broadcasted_iota(jnp.int32, (GROUP_SIZE, tm), 0).astype(F32)
    g_rows = []
    for g in range(N_GROUPS):
        seg = sel[g * GROUP_SIZE:(g + 1) * GROUP_SIZE, :]
        m1 = jnp.max(seg, axis=0, keepdims=True)
        first = _first_index(seg == m1, gi, float(GROUP_SIZE))
        m2 = jnp.max(jnp.where(gi == first, ninf, seg), axis=0, keepdims=True)
        g_rows.append(m1 + m2)
    g_score = jnp.concatenate(g_rows, axis=0)

    ggi = lax.broadcasted_iota(jnp.int32, (N_GROUPS, tm), 0).astype(F32)
    g_keep = jnp.zeros((N_GROUPS, tm), F32)
    cur = g_score
    for _ in range(TOPK_GROUPS):
        mx = jnp.max(cur, axis=0, keepdims=True)
        hit = ggi == _first_index(cur == mx, ggi, float(N_GROUPS))
        g_keep = jnp.where(hit, 1.0, g_keep)
        cur = jnp.where(hit, ninf, cur)

    keep = jnp.concatenate(
        [jnp.broadcast_to(g_keep[g:g + 1, :], (GROUP_SIZE, tm)) for g in range(N_GROUPS)], axis=0)
    cur = jnp.where(keep > 0.0, sel, NEG)
    ei = lax.broadcasted_iota(jnp.int32, (N_EXPERTS, tm), 0).astype(F32)
    idx_rows, w_rows = [], []
    chosen = jnp.zeros((N_EXPERTS, tm), F32)
    for _ in range(TOP_K):
        mx = jnp.max(cur, axis=0, keepdims=True)
        first = _first_index(cur == mx, ei, float(N_EXPERTS))
        hit = ei == first
        idx_rows.append(first)
        w_rows.append(jnp.sum(jnp.where(hit, scores, 0.0), axis=0, keepdims=True))
        cur = jnp.where(hit, ninf, cur)
        chosen = jnp.where(hit, 1.0, chosen)
    w = jnp.concatenate(w_rows, axis=0)
    idx_ref[...] = jnp.concatenate(idx_rows, axis=0).astype(jnp.int32)
    wts_ref[...] = w / jnp.sum(w, axis=0, keepdims=True) * ROUTED_SCALE

    t_src = lax.broadcasted_iota(jnp.int32, (tm, tm), 0)
    t_dst = lax.broadcasted_iota(jnp.int32, (tm, tm), 1)
    before = (t_src < t_dst).astype(BF16)
    rank = _dot(chosen.astype(BF16), before) + run_ref[...]
    pos_ref[...] = jnp.concatenate(
        [jnp.sum(jnp.where(ei == r, rank, 0.0), axis=0, keepdims=True) for r in idx_rows],
        axis=0).astype(jnp.int32)
    run = run_ref[...] + jnp.sum(chosen, axis=1, keepdims=True)
    run_ref[...] = run
    cnt_ref[...] = run.astype(jnp.int32)


def _router_call(x, w_router, router_bias):
    t, d = x.shape
    tm = min(512, t)
    wt = w_router.T
    wh = wt.astype(BF16)
    wl = (wt - wh.astype(F32)).astype(BF16)
    kt = lambda: pl.BlockSpec((TOP_K, tm), lambda i: (0, i))
    return pl.pallas_call(
        _router_kernel,
        grid=(t // tm,),
        in_specs=[pl.BlockSpec((tm, d), lambda i: (i, 0)),
                  pl.BlockSpec((N_EXPERTS, d), lambda i: (0, 0)),
                  pl.BlockSpec((N_EXPERTS, d), lambda i: (0, 0)),
                  pl.BlockSpec((N_EXPERTS, 1), lambda i: (0, 0))],
        out_specs=[kt(), kt(), kt(), pl.BlockSpec((N_EXPERTS, 1), lambda i: (0, 0))],
        out_shape=[jax.ShapeDtypeStruct((TOP_K, t), jnp.int32),
                   jax.ShapeDtypeStruct((TOP_K, t), F32),
                   jax.ShapeDtypeStruct((TOP_K, t), jnp.int32),
                   jax.ShapeDtypeStruct((N_EXPERTS, 1), jnp.int32)],
        scratch_shapes=[pltpu.VMEM((N_EXPERTS, 1), F32)],
        compiler_params=_cparams("arbitrary"),
        name="router",
    )(x, wh, wl, router_bias.astype(F32).reshape(N_EXPERTS, 1))


def _dest_kernel(idx_ref, pos_ref, start_ref, dest_ref):
    tm = idx_ref.shape[1]
    ei = lax.broadcasted_iota(jnp.int32, (N_EXPERTS, tm), 0)
    start = start_ref[...].astype(F32)
    rows = [jnp.sum(jnp.where(ei == idx_ref[k:k + 1, :], start, 0.0), axis=0, keepdims=True)
            for k in range(TOP_K)]
    dest_ref[...] = pos_ref[...] + jnp.concatenate(rows, axis=0).astype(jnp.int32)


def _dest_call(idx_kt, pos_kt, p_start):
    t = idx_kt.shape[1]
    tm = min(1024, t)
    kt = lambda: pl.BlockSpec((TOP_K, tm), lambda i: (0, i))
    return pl.pallas_call(
        _dest_kernel,
        grid=(t // tm,),
        in_specs=[kt(), kt(), pl.BlockSpec((N_EXPERTS, 1), lambda i: (0, 0))],
        out_specs=kt(),
        out_shape=jax.ShapeDtypeStruct((TOP_K, t), jnp.int32),
        compiler_params=_cparams("parallel"),
        name="moe_dest",
    )(idx_kt, pos_kt, p_start.reshape(N_EXPERTS, 1))


LANES = 128
ROW_SUB = D_MODEL // LANES
DISPATCH_TILE = 256


def _row_tile(ref, r):
    return ref.at[pl.ds(pl.multiple_of(r * ROW_SUB, ROW_SUB), ROW_SUB), :]


def _lane_chunk(first_row, n_rows, j):
    return pl.ds(first_row * ROW_SUB + j, n_rows, stride=ROW_SUB)


def _to_row_tiles(dst_ref, v):
    for j in range(ROW_SUB):
        dst_ref[_lane_chunk(0, v.shape[0], j), :] = v[:, j * LANES:(j + 1) * LANES]


def _from_row_tiles(src_ref, first_row, n_rows):
    return jnp.concatenate([src_ref[_lane_chunk(first_row, n_rows, j), :] for j in range(ROW_SUB)], axis=1)


def _dispatch_kernel(valid_ref, dest_ref, x_ref, xs_hbm, xt, zbuf, sem, zsem):
    tm = xt.shape[0] // ROW_SUB
    bm = zbuf.shape[0] // ROW_SUB
    n_blk = valid_ref.shape[0]

    @pl.when(pl.program_id(0) == 0)
    def _():
        zbuf[...] = jnp.zeros_like(zbuf)

        def zero_copy(i):
            first = pl.multiple_of(i * (bm * ROW_SUB), bm * ROW_SUB)
            return pltpu.make_async_copy(zbuf, xs_hbm.at[pl.ds(first, bm * ROW_SUB), :], zsem)

        def start(i, carry):
            @pl.when(valid_ref[i] < bm)
            def _():
                zero_copy(i).start()
            return carry

        def wait(i, carry):
            @pl.when(valid_ref[i] < bm)
            def _():
                zero_copy(i).wait()
            return carry

        lax.fori_loop(0, n_blk, start, 0)
        lax.fori_loop(0, n_blk, wait, 0)

    _to_row_tiles(xt, x_ref[...])

    def body(r, carry):
        for k in range(TOP_K):
            pltpu.make_async_copy(_row_tile(xt, r), _row_tile(xs_hbm, dest_ref[k, r]), sem).start()
        return carry

    lax.fori_loop(0, tm, body, 0)
    for k in range(TOP_K):
        pltpu.make_async_copy(xt, xs_hbm.at[pl.ds(0, tm * ROW_SUB), :], sem).wait()


def _dispatch_call(x, dest_kt, blk_valid):
    t, d = x.shape
    tm = min(DISPATCH_TILE, t)
    bm = MOE_BLOCK
    n_blk = blk_valid.shape[0]
    grid_spec = pltpu.PrefetchScalarGridSpec(
        num_scalar_prefetch=1,
        grid=(t // tm,),
        in_specs=[pl.BlockSpec((TOP_K, tm), lambda i, v: (0, i), memory_space=pltpu.SMEM),
                  pl.BlockSpec((tm, d), lambda i, v: (i, 0))],
        out_specs=pl.BlockSpec(memory_space=pl.ANY),
        scratch_shapes=[pltpu.VMEM((tm * ROW_SUB, LANES), F32), pltpu.VMEM((bm * ROW_SUB, LANES), F32),
                        pltpu.SemaphoreType.DMA(()), pltpu.SemaphoreType.DMA(())],
    )
    return pl.pallas_call(
        _dispatch_kernel,
        grid_spec=grid_spec,
        out_shape=jax.ShapeDtypeStruct((n_blk * bm * ROW_SUB, LANES), F32),
        compiler_params=_cparams("arbitrary"),
        name="moe_dispatch",
    )(blk_valid, dest_kt, x)


def _expert_kernel(blk_e_ref, nused_ref, xs_ref, w13_ref, w2_ref, o_ref):
    i = pl.program_id(0)

    @pl.when(i < nused_ref[0])
    def _():
        xb = _from_row_tiles(xs_ref, 0, xs_ref.shape[0] // ROW_SUB).astype(BF16)
        h13 = _dot(xb, w13_ref[0])
        hid = _silu(h13[:, :D_EXPERT]) * h13[:, D_EXPERT:]
        _to_row_tiles(o_ref, _dot(hid.astype(BF16), w2_ref[0]))

    @pl.when(i >= nused_ref[0])
    def _():
        o_ref[...] = jnp.zeros_like(o_ref)


def _expert_call(xs, blk_e, n_used, w13, w2):
    bm = MOE_BLOCK
    n_blk = blk_e.shape[0]
    d = D_MODEL
    xs_idx = lambda i, e, n: (jnp.minimum(i, jnp.maximum(n[0] - 1, 0)), 0)
    grid_spec = pltpu.PrefetchScalarGridSpec(
        num_scalar_prefetch=2,
        grid=(n_blk,),
        in_specs=[pl.BlockSpec((bm * ROW_SUB, LANES), xs_idx),
                  pl.BlockSpec((1, d, 2 * D_EXPERT), lambda i, e, n: (e[i], 0, 0)),
                  pl.BlockSpec((1, D_EXPERT, d), lambda i, e, n: (e[i], 0, 0))],
        out_specs=pl.BlockSpec((bm * ROW_SUB, LANES), lambda i, e, n: (i, 0)),
    )
    return pl.pallas_call(
        _expert_kernel,
        grid_spec=grid_spec,
        out_shape=jax.ShapeDtypeStruct((n_blk * bm * ROW_SUB, LANES), F32),
        compiler_params=_cparams("arbitrary"),
        name="routed_experts",
    )(blk_e, n_used, xs, w13, w2)


def _combine_kernel(dest_ref, destn_ref, ys_hbm, w_ref, dense_ref, x_ref, g_ref, b_ref,
                    o_ref, ob_ref, gbuf, sem, *, alpha):
    i = pl.program_id(0)
    n = pl.num_programs(0)
    rows = gbuf.shape[1] // ROW_SUB
    tm = rows // TOP_K
    cur = i % 2

    def issue(d_ref, s):
        def body(r, carry):
            for k in range(TOP_K):
                pltpu.make_async_copy(_row_tile(ys_hbm, d_ref[k, r]), _row_tile(gbuf.at[s], k * tm + r),
                                      sem.at[s]).start()
            return carry
        lax.fori_loop(0, tm, body, 0)

    @pl.when(i == 0)
    def _():
        issue(dest_ref, 0)

    @pl.when(i + 1 < n)
    def _():
        issue(destn_ref, 1 - cur)

    pltpu.make_async_copy(ys_hbm.at[pl.ds(0, rows * ROW_SUB), :], gbuf.at[cur], sem.at[cur]).wait()
    w = w_ref[...]
    routed = _from_row_tiles(gbuf.at[cur], 0, tm) * w[:, 0:1]
    for k in range(1, TOP_K):
        routed = routed + _from_row_tiles(gbuf.at[cur], k * tm, tm) * w[:, k:k + 1]
    y = _layer_norm_rows(alpha * x_ref[...] + routed + dense_ref[...], g_ref[...], b_ref[...])
    o_ref[...] = y
    ob_ref[...] = y.astype(BF16)


def _combine_call(ys, dest_kt, wts, dense, x, g, b, alpha):
    t, d = x.shape
    tm = min(COMBINE_TILE, t)
    nt = t // tm
    rows = tm * TOP_K
    tile = lambda: pl.BlockSpec((tm, d), lambda i: (i, 0))
    vec = lambda: pl.BlockSpec((1, d), lambda i: (0, 0))
    return pl.pallas_call(
        functools.partial(_combine_kernel, alpha=alpha),
        grid=(nt,),
        in_specs=[pl.BlockSpec((TOP_K, tm), lambda i: (0, i), memory_space=pltpu.SMEM),
                  pl.BlockSpec((TOP_K, tm), lambda i: (0, jnp.minimum(i + 1, nt - 1)),
                               memory_space=pltpu.SMEM),
                  pl.BlockSpec(memory_space=pl.ANY),
                  pl.BlockSpec((tm, TOP_K), lambda i: (i, 0)),
                  tile(), tile(), vec(), vec()],
        out_specs=[tile(), tile()],
        out_shape=[jax.ShapeDtypeStruct((t, d), F32), jax.ShapeDtypeStruct((t, d), BF16)],
        scratch_shapes=[pltpu.VMEM((2, rows * ROW_SUB, LANES), F32), pltpu.SemaphoreType.DMA((2,))],
        compiler_params=_cparams("arbitrary"),
        name="combine_ln2",
    )(dest_kt, dest_kt, ys, wts, dense, x, g.reshape(1, d), b.reshape(1, d))


def _block_tables(counts, n_blk):
    bm = MOE_BLOCK
    padded = (counts + bm - 1) // bm * bm
    p_end = jnp.cumsum(padded)
    p_start = p_end - padded
    blk_row = jnp.arange(n_blk, dtype=jnp.int32) * bm
    blk_e = jnp.minimum(jnp.searchsorted(p_end, blk_row, side='right'), N_EXPERTS - 1).astype(jnp.int32)
    blk_valid = jnp.clip(counts[blk_e] - (blk_row - p_start[blk_e]), 0, bm).astype(jnp.int32)
    n_used = (p_end[-1] // bm).astype(jnp.int32).reshape(1)
    return p_start.astype(jnp.int32), blk_e, blk_valid, n_used


def _prep_layer(li, lower, w_in, b_in, conv_w, conv_b, norm_a, norm_b, w_out_a, w_out_b, w_o,
                ln1_g, ln1_b, w_ple, w_ple_gate, w_router, router_bias, w1, w3, w2, ws1, ws3, ws2,
                ln2_g, ln2_b):
    d = D_MODEL
    w = w_in[li]
    bi = b_in[li]
    seg = lambda a, lo, hi: a[..., lo:hi]
    o_qkv, o_oa, o_g = 0, 3 * d, 4 * d
    o_qb = 4 * d + 4 * H_A
    o_ff, o_ib, o_gb, o_ga = o_qb + d, o_qb + 3 * d, o_qb + 4 * d, o_qb + 5 * d
    cat = lambda parts: jnp.concatenate(parts, axis=-1)
    raw_cols = [(o_qkv, o_qkv + 3 * d), (o_ib, o_ib + d)]
    sig_cols = [(o_oa, o_oa + d), (o_ga, o_ga + 2 * d)]
    silu_cols = [(o_qb, o_qb + d), (o_gb, o_gb + d)]
    return dict(
        w_raw=cat([seg(w, *c) for c in raw_cols]).astype(BF16), b_raw=cat([seg(bi, *c) for c in raw_cols]),
        w_sig=cat([seg(w, *c) for c in sig_cols]).astype(BF16), b_sig=cat([seg(bi, *c) for c in sig_cols]),
        w_silu=cat([seg(w, *c) for c in silu_cols]).astype(BF16), b_silu=cat([seg(bi, *c) for c in silu_cols]),
        w_fg=seg(w, o_ff, o_ff + 2 * d).astype(BF16), b_fg=seg(bi, o_ff, o_ff + 2 * d),
        w_g=seg(w, o_g, o_g + 4 * H_A).astype(BF16), b_g=seg(bi, o_g, o_g + 4 * H_A),
        lower=lower[li].reshape(-1),
        conv_w=conv_w[li], conv_b=conv_b[li], norm_a=norm_a[li], norm_b=norm_b[li],
        w_out_a=w_out_a[li].astype(BF16), w_out_b=w_out_b[li].astype(BF16), w_o=w_o[li].astype(BF16),
        ln1_g=ln1_g[li], ln1_b=ln1_b[li],
        w_ple=w_ple[li].astype(BF16), w_ple_gate=w_ple_gate[li].astype(BF16),
        w_router=w_router[li], router_bias=router_bias[li],
        w13=jnp.concatenate([w1[li], w3[li]], axis=-1).astype(BF16), w2=w2[li].astype(BF16),
        ws13=jnp.concatenate([ws1[li], ws3[li]], axis=-1).astype(BF16), ws2=ws2[li].astype(BF16),
        ln2_g=ln2_g[li], ln2_b=ln2_b[li],
    )


def _layer(x, xb, p, lp, bsz, s, alpha):
    t, d = x.shape
    a = _proj_act(xb, lp["w_raw"], lp["b_raw"], "raw", "proj_raw")
    sg = _proj_act(xb, lp["w_sig"], lp["b_sig"], "sigmoid", "proj_sigmoid")
    sl = _proj_act(xb, lp["w_silu"], lp["b_silu"], "silu", "proj_silu")
    kf, lf = _proj_forget(xb, lp["w_fg"], lp["b_fg"], lp["lower"])
    gc, gr = _proj_gates(xb, lp["w_g"], lp["b_g"])

    a3 = a.reshape(bsz, s, 4 * d)
    qk3 = _conv_silu(a3, lp["conv_w"], lp["conv_b"])
    gc3 = gc.reshape(bsz, s, 4 * H_A)
    haf = _mlstm_dir(qk3, a3, gc3, gr, False).reshape(t, d)
    hab = _mlstm_dir(qk3, a3, gc3, gr, True).reshape(t, d)

    sl3 = sl.reshape(bsz, s, 2 * d)
    kf3 = kf.reshape(bsz, s, 2 * d)
    lf3 = lf.reshape(bsz, s, 2 * d)
    hbf = _hgrn_dir(sl3, kf3, a3, lf3, False).reshape(t, d)
    hbb = _hgrn_dir(sl3, kf3, a3, lf3, True).reshape(t, d)

    x1, x1b = _mix_call(x, haf, hab, hbf, hbb, sg, sl, lp["norm_a"], lp["norm_b"],
                        lp["w_out_a"], lp["w_out_b"], lp["w_o"], lp["ln1_g"], lp["ln1_b"], alpha)

    dense = _dense_call(x1b, p, lp["ws13"], lp["ws2"], lp["w_ple"], lp["w_ple_gate"])
    idx_kt, wts_kt, pos_kt, counts = _router_call(x1, lp["w_router"], lp["router_bias"])
    n_blk = t * TOP_K // MOE_BLOCK + N_EXPERTS
    p_start, blk_e, blk_valid, n_used = _block_tables(counts.reshape(-1), n_blk)
    dest_kt = _dest_call(idx_kt, pos_kt, p_start)
    xs = _dispatch_call(x1, dest_kt, blk_valid)
    ys = _expert_call(xs, blk_e, n_used, lp["w13"], lp["w2"])
    return _combine_call(ys, dest_kt, wts_kt.T, dense, x1, lp["ln2_g"], lp["ln2_b"], alpha)


def kernel(x_prompt, x_sample, p_prompt, p_sample, emb_ln_g, emb_ln_b, w_in, b_in, conv_w, conv_b, norm_a, norm_b, lb_param, w_out_a, w_out_b, w_o, ln1_g, ln1_b, w_ple, w_ple_gate, w_router, router_bias, w1, w3, w2, ws1, ws3, ws2, ln2_g, ln2_b):
    depth = w_in.shape[0]
    alpha = (2.0 * depth) ** 0.25
    lb_soft = jax.nn.softmax(lb_param.astype(F32), axis=0)
    lower = jnp.cumsum(lb_soft, axis=0) - lb_soft[0:1]
    layers = [_prep_layer(li, lower, w_in, b_in, conv_w, conv_b, norm_a, norm_b, w_out_a, w_out_b, w_o,
                          ln1_g, ln1_b, w_ple, w_ple_gate, w_router, router_bias, w1, w3, w2,
                          ws1, ws3, ws2, ln2_g, ln2_b) for li in range(depth)]

    def trunk(x3, p4):
        bsz, s, d = x3.shape
        x, xb = _layer_norm_call(x3.reshape(bsz * s, d), emb_ln_g, emb_ln_b)
        for li in range(depth):
            x, xb = _layer(x, xb, p4[li].reshape(bsz * s, -1), layers[li], bsz, s, alpha)
        return x.reshape(bsz, s, d)

    return (trunk(x_prompt, p_prompt), trunk(x_sample, p_sample))
```

```python
import functools

import jax
import jax.numpy as jnp
from jax import lax
from jax.experimental import pallas as pl
from jax.experimental.pallas import tpu as pltpu

F32 = jnp.float32
BF16 = jnp.bfloat16

D_MODEL = 1024
PLE_DIM = 256
H_A = 4
DH_A = D_MODEL // H_A
CHUNK_A = 128
H_B = 8
DK_B = 128
CHUNK_B = 128
N_EXPERTS = 256
TOP_K = 8
N_GROUPS = 8
GROUP_SIZE = N_EXPERTS // N_GROUPS
TOPK_GROUPS = 4
D_EXPERT = 256
ROUTED_SCALE = 2.5
EPS = 1e-5
NEG = -1e30
TINY = 1e-30

HGRN_SAFE_RANGE = 80.0
HGRN_ROWS = CHUNK_B
MOE_BLOCK = 512
COMBINE_TILE = 128
VMEM_LIMIT = 56 * 1024 * 1024


def _cparams(*sem):
    return pltpu.CompilerParams(dimension_semantics=sem, vmem_limit_bytes=VMEM_LIMIT)


def _split3(x):
    hi = x.astype(BF16)
    r1 = x - hi.astype(F32)
    mid = r1.astype(BF16)
    lo = (r1 - mid.astype(F32)).astype(BF16)
    return hi, mid, lo


def _dot(a, b):
    return jnp.dot(a, b, preferred_element_type=F32)


def _dot_nt(a, b):
    return lax.dot_general(a, b, (((1,), (1,)), ((), ())), preferred_element_type=F32)


def _dot_tn(a, b):
    return lax.dot_general(a, b, (((0,), (0,)), ((), ())), preferred_element_type=F32)


def _layer_norm_rows(v, g, b):
    mu = jnp.mean(v, axis=-1, keepdims=True)
    vc = v - mu
    var = jnp.mean(vc * vc, axis=-1, keepdims=True)
    return vc * lax.rsqrt(var + EPS) * g + b


def _log_sigmoid(z):
    return jnp.minimum(z, 0.0) - jnp.log1p(jnp.exp(-jnp.abs(z)))


def _silu(z):
    return z * jax.nn.sigmoid(z)


def _ln_kernel(x_ref, g_ref, b_ref, o_ref, ob_ref):
    y = _layer_norm_rows(x_ref[...], g_ref[...], b_ref[...])
    o_ref[...] = y
    ob_ref[...] = y.astype(BF16)


def _layer_norm_call(x, g, b):
    t, d = x.shape
    tm = min(512, t)
    return pl.pallas_call(
        _ln_kernel,
        grid=(t // tm,),
        in_specs=[pl.BlockSpec((tm, d), lambda i: (i, 0)),
                  pl.BlockSpec((1, d), lambda i: (0, 0)),
                  pl.BlockSpec((1, d), lambda i: (0, 0))],
        out_specs=[pl.BlockSpec((tm, d), lambda i: (i, 0)),
                   pl.BlockSpec((tm, d), lambda i: (i, 0))],
        out_shape=[jax.ShapeDtypeStruct((t, d), F32), jax.ShapeDtypeStruct((t, d), BF16)],
        compiler_params=_cparams("parallel"),
        name="emb_layer_norm",
    )(x, g.reshape(1, d), b.reshape(1, d))


PROJ_SUB = 256


def _row_blocks(n):
    step = min(PROJ_SUB, n)
    return [slice(r, r + step) for r in range(0, n, step)]


def _proj_act_kernel(x_ref, w_ref, b_ref, o_ref, *, act):
    for rs in _row_blocks(x_ref.shape[0]):
        z = _dot(x_ref[rs, :], w_ref[...]) + b_ref[...]
        if act == "sigmoid":
            z = jax.nn.sigmoid(z)
        elif act == "silu":
            z = _silu(z)
        o_ref[rs, :] = z.astype(o_ref.dtype)


def _proj_act(xb, w, b, act, name):
    t, d = xb.shape
    n = w.shape[1]
    tm = min(1024, t)
    tn = 1024
    return pl.pallas_call(
        functools.partial(_proj_act_kernel, act=act),
        grid=(t // tm, n // tn),
        in_specs=[pl.BlockSpec((tm, d), lambda i, j: (i, 0)),
                  pl.BlockSpec((d, tn), lambda i, j: (0, j)),
                  pl.BlockSpec((1, tn), lambda i, j: (0, j))],
        out_specs=pl.BlockSpec((tm, tn), lambda i, j: (i, j)),
        out_shape=jax.ShapeDtypeStruct((t, n), BF16),
        compiler_params=_cparams("parallel", "arbitrary"),
        name=name,
    )(xb, w, b.reshape(1, n))


def _proj_forget_kernel(x_ref, w_ref, b_ref, oml_ref, la_ref, lc_ref, k_ref, lf_ref):
    a = la_ref[...]
    for rs in _row_blocks(x_ref.shape[0]):
        z = _dot(x_ref[rs, :], w_ref[...]) + b_ref[...]
        e = jnp.exp(-jnp.abs(z))
        r = 1.0 / (1.0 + e)
        k_ref[rs, :] = (oml_ref[...] * (jnp.where(z > 0.0, e, 1.0) * r)).astype(BF16)
        u = lc_ref[...] + (jnp.minimum(z, 0.0) + jnp.log(r))
        lf_ref[rs, :] = jnp.maximum(a, u) + jnp.log(1.0 + jnp.exp(-jnp.abs(a - u)))


def _proj_forget(xb, w, b, lower):
    t, d = xb.shape
    n = w.shape[1]
    tm = min(1024, t)
    tn = 1024
    lb = lower.reshape(1, n)
    vec = lambda: pl.BlockSpec((1, tn), lambda i, j: (0, j))
    return pl.pallas_call(
        _proj_forget_kernel,
        grid=(t // tm, n // tn),
        in_specs=[pl.BlockSpec((tm, d), lambda i, j: (i, 0)),
                  pl.BlockSpec((d, tn), lambda i, j: (0, j)),
                  vec(), vec(), vec(), vec()],
        out_specs=[pl.BlockSpec((tm, tn), lambda i, j: (i, j)),
                   pl.BlockSpec((tm, tn), lambda i, j: (i, j))],
        out_shape=[jax.ShapeDtypeStruct((t, n), BF16), jax.ShapeDtypeStruct((t, n), F32)],
        compiler_params=_cparams("parallel", "arbitrary"),
        name="proj_forget",
    )(xb, w, b.reshape(1, n), 1.0 - lb, jnp.log(jnp.maximum(lb, TINY)), jnp.log1p(-lb))


def _proj_gates_kernel(x_ref, w_ref, wt_ref, b_ref, bt_ref, gc_ref, gr_ref):
    x = x_ref[...]
    zc = _dot(x, w_ref[...]) + b_ref[...]
    lane = lax.broadcasted_iota(jnp.int32, zc.shape, 1)
    gc_ref[...] = jnp.where(lane >= 2 * H_A, _log_sigmoid(zc), zc)
    zr = _dot_nt(wt_ref[...], x) + bt_ref[...]
    sub = lax.broadcasted_iota(jnp.int32, zr.shape, 0)
    gr_ref[...] = jnp.where(sub >= 2 * H_A, _log_sigmoid(zr), zr)


def _proj_gates(xb, w, b):
    t, d = xb.shape
    n = w.shape[1]
    tm = min(1024, t)
    return pl.pallas_call(
        _proj_gates_kernel,
        grid=(t // tm,),
        in_specs=[pl.BlockSpec((tm, d), lambda i: (i, 0)),
                  pl.BlockSpec((d, n), lambda i: (0, 0)),
                  pl.BlockSpec((n, d), lambda i: (0, 0)),
                  pl.BlockSpec((1, n), lambda i: (0, 0)),
                  pl.BlockSpec((n, 1), lambda i: (0, 0))],
        out_specs=[pl.BlockSpec((tm, n), lambda i: (i, 0)),
                   pl.BlockSpec((n, tm), lambda i: (0, i))],
        out_shape=[jax.ShapeDtypeStruct((t, n), F32), jax.ShapeDtypeStruct((n, t), F32)],
        compiler_params=_cparams("parallel"),
        name="proj_gates",
    )(xb, w, w.T, b.reshape(1, n), b.reshape(n, 1))


def _conv_kernel(x_ref, w_ref, b_ref, o_ref, *, rows):
    s = x_ref.shape[1]
    n_rc = s // rows
    w = w_ref[...]
    w0, w1, w2 = w[0:1, :], w[1:2, :], w[2:3, :]
    bias = b_ref[...]
    ridx = lax.broadcasted_iota(jnp.int32, (rows, x_ref.shape[2]), 0)

    def body(r, carry):
        base = pl.multiple_of(r * rows, rows)
        cur = x_ref[0, pl.ds(base, rows), :].astype(F32)
        pstart = pl.multiple_of(jnp.maximum(base - 16, 0), 16)
        nstart = pl.multiple_of(jnp.minimum(base + rows, s - 16), 16)
        prev_row = x_ref[0, pl.ds(pstart, 16), :][15:16, :].astype(F32)
        next_row = x_ref[0, pl.ds(nstart, 16), :][0:1, :].astype(F32)
        prev_row = jnp.where(r > 0, prev_row, 0.0)
        next_row = jnp.where(r < n_rc - 1, next_row, 0.0)
        up = jnp.where(ridx == 0, prev_row, pltpu.roll(cur, 1, 0))
        dn = jnp.where(ridx == rows - 1, next_row, pltpu.roll(cur, rows - 1, 0))
        z = w0 * up + w1 * cur + w2 * dn + bias
        o_ref[0, pl.ds(base, rows), :] = _silu(z).astype(o_ref.dtype)
        return carry

    lax.fori_loop(0, n_rc, body, 0)


def _conv_silu(a3, conv_w, conv_b):
    bsz, s, _ = a3.shape
    c = conv_w.shape[1]
    cw = 256
    rows = min(256, s)
    return pl.pallas_call(
        functools.partial(_conv_kernel, rows=rows),
        grid=(bsz, c // cw),
        in_specs=[pl.BlockSpec((1, s, cw), lambda b, j: (b, 0, j)),
                  pl.BlockSpec((3, cw), lambda b, j: (0, j)),
                  pl.BlockSpec((1, cw), lambda b, j: (0, j))],
        out_specs=pl.BlockSpec((1, s, cw), lambda b, j: (b, 0, j)),
        out_shape=jax.ShapeDtypeStruct((bsz, s, c), BF16),
        compiler_params=_cparams("parallel", "parallel"),
        name="conv_silu",
    )(a3, conv_w, conv_b.reshape(1, c))


def _mlstm_kernel(q_ref, k_ref, v_ref, gc_ref, gr_ref, o_ref, c_ref, n_ref, m_ref, *, reverse):
    L = CHUNK_A

    @pl.when(pl.program_id(1) == 0)
    def _():
        c_ref[...] = jnp.zeros_like(c_ref)
        n_ref[...] = jnp.zeros_like(n_ref)
        m_ref[...] = jnp.zeros_like(m_ref)

    row = lax.broadcasted_iota(jnp.int32, (L, L), 0)
    col = lax.broadcasted_iota(jnp.int32, (L, L), 1)
    if reverse:
        mask = col >= row
        mask_t = row >= col
        last = 0
    else:
        mask = col <= row
        mask_t = row <= col
        last = L - 1
    m_mat = mask.astype(BF16)
    mt_mat = mask_t.astype(BF16)

    gc = gc_ref[0]
    gr = gr_ref[...]
    cum_c = sum(_dot(m_mat, p) for p in _split3(gc))
    cum_r = sum(_dot(p, mt_mat) for p in _split3(gr))
    off = H_A if reverse else 0
    scale = DH_A ** -0.5

    for h in range(H_A):
        ii = off + h
        fi = 2 * H_A + off + h
        hs = slice(h * DH_A, (h + 1) * DH_A)
        bc = cum_c[:, fi:fi + 1]
        br = cum_r[fi:fi + 1, :]
        ic = gc[:, ii:ii + 1]
        ir = gr[ii:ii + 1, :]
        m_st = m_ref[h:h + 1, 0:1]
        qh = q_ref[0, :, hs]
        kh = k_ref[0, :, hs]
        vh = v_ref[0, :, hs]
        c_st = c_ref[h]
        n_st = n_ref[h:h + 1, :]

        d_mat = jnp.where(mask, bc - br + ir, NEG)
        m_inter = bc + m_st
        m_row = jnp.maximum(m_inter, jnp.max(d_mat, axis=1, keepdims=True))
        w_inter = jnp.exp(m_inter - m_row)
        s_mat = _dot_nt(qh, kh) * scale * jnp.exp(d_mat - m_row)
        num = w_inter * _dot(qh, c_st.astype(BF16)) + _dot(s_mat.astype(BF16), vh)
        qn = jnp.sum(qh.astype(F32) * n_st, axis=1, keepdims=True)
        den = w_inter * qn + jnp.sum(s_mat, axis=1, keepdims=True)
        inv = 1.0 / jnp.maximum(jnp.abs(den), jnp.exp(-m_row))
        o_ref[0, :, hs] = (num * inv).astype(o_ref.dtype)

        b_last = bc[last:last + 1, :]
        g_end_r = b_last - br + ir
        g_end_c = b_last - bc + ic
        m_next = jnp.maximum(b_last + m_st, jnp.max(g_end_r, axis=1, keepdims=True))
        decay = jnp.exp(b_last + m_st - m_next)
        wk = jnp.exp(g_end_c - m_next) * (kh.astype(F32) * scale)
        c_ref[h] = decay * c_st + _dot_tn(wk.astype(BF16), vh)
        n_ref[h:h + 1, :] = decay * n_st + jnp.sum(wk, axis=0, keepdims=True)
        m_ref[h:h + 1, :] = jnp.broadcast_to(m_next, (1, m_ref.shape[1]))


def _mlstm_dir(qk3, a3, gc3, gr, reverse):
    bsz, s, _ = qk3.shape
    L = CHUNK_A
    nc = s // L
    cidx = (lambda c: nc - 1 - c) if reverse else (lambda c: c)
    return pl.pallas_call(
        functools.partial(_mlstm_kernel, reverse=reverse),
        grid=(bsz, nc),
        in_specs=[pl.BlockSpec((1, L, D_MODEL), lambda b, c: (b, cidx(c), 0)),
                  pl.BlockSpec((1, L, D_MODEL), lambda b, c: (b, cidx(c), 1)),
                  pl.BlockSpec((1, L, D_MODEL), lambda b, c: (b, cidx(c), 2)),
                  pl.BlockSpec((1, L, 4 * H_A), lambda b, c: (b, cidx(c), 0)),
                  pl.BlockSpec((4 * H_A, L), lambda b, c: (0, b * nc + cidx(c)))],
        out_specs=pl.BlockSpec((1, L, D_MODEL), lambda b, c: (b, cidx(c), 0)),
        out_shape=jax.ShapeDtypeStruct((bsz, s, D_MODEL), BF16),
        scratch_shapes=[pltpu.VMEM((H_A, DH_A, DH_A), F32),
                        pltpu.VMEM((H_A, DH_A), F32),
                        pltpu.VMEM((H_A, 128), F32)],
        compiler_params=_cparams("parallel", "arbitrary"),
        name="mlstm_bwd" if reverse else "mlstm_fwd",
    )(qk3, qk3, a3, gc3, gr)


def _hgrn_kernel(q_ref, k_ref, v_ref, lf_ref, o_ref, st_ref, oi_ref, b_ref, q32_ref, k32_ref, v32_ref, *,
                 reverse):
    L = CHUNK_B
    rows_blk = q_ref.shape[1]
    n_sub = rows_blk // L
    width = q_ref.shape[2]

    @pl.when(pl.program_id(1) == 0)
    def _():
        st_ref[...] = jnp.zeros_like(st_ref)

    row = lax.broadcasted_iota(jnp.int32, (rows_blk, rows_blk), 0)
    col = lax.broadcasted_iota(jnp.int32, (rows_blk, rows_blk), 1)
    vis = (col >= row) if reverse else (col <= row)
    same = (row // L) == (col // L)
    last = 0 if reverse else L - 1
    mask = vis[0:L, 0:L]

    m_all = jnp.logical_and(vis, same).astype(BF16)
    b = sum(_dot(m_all, p) for p in _split3(lf_ref[0]))

    def per_chunk_row(r):
        return jnp.concatenate(
            [jnp.broadcast_to(b[s * L + r:s * L + r + 1, :], (L, width)) for s in range(n_sub)], axis=0)

    q = q_ref[0].astype(F32)
    k = k_ref[0].astype(F32)
    v = v_ref[0]
    q_in = (q * jnp.exp(b)).astype(BF16)
    k_out = (k * jnp.exp(per_chunk_row(last) - b)).astype(BF16)
    dev = b - per_chunk_row(L // 2)
    rng = jnp.max(jnp.abs(dev))
    qt = (q * jnp.exp(dev)).astype(BF16)
    kt = (k * jnp.exp(-dev)).astype(BF16)

    for s in (reversed(range(n_sub)) if reverse else range(n_sub)):
        rs = slice(s * L, (s + 1) * L)
        dec = jnp.exp(b[s * L + last:s * L + last + 1, :])
        for h in range(H_B):
            hs = slice(h * DK_B, (h + 1) * DK_B)
            st = st_ref[h]
            o_inter = _dot_nt(q_in[rs, hs], st.astype(BF16))
            a = jnp.where(mask, _dot_nt(qt[rs, hs], kt[rs, hs]), 0.0)
            o_ref[0, rs, hs] = (o_inter + _dot(a.astype(BF16), v[rs, hs])).astype(o_ref.dtype)
            oi_ref[rs, hs] = o_inter
            st_ref[h] = dec[:, hs] * st + _dot_tn(v[rs, hs], k_out[rs, hs])

    @pl.when(rng > HGRN_SAFE_RANGE)
    def _():
        b_ref[...] = b
        q32_ref[...] = q
        k32_ref[...] = k
        v32_ref[...] = v.astype(F32)
        sidx = lax.broadcasted_iota(jnp.int32, (rows_blk, 1), 0)

        def body(j, carry):
            is_j = sidx == j
            b_all = b_ref[...]
            bj = jnp.sum(jnp.where(is_j, b_all, 0.0), axis=0, keepdims=True)
            qj = jnp.sum(jnp.where(is_j, q32_ref[...], 0.0), axis=0, keepdims=True)
            valid = jnp.logical_and((sidx >= j) if reverse else (sidx <= j), sidx // L == j // L)
            e = jnp.exp(jnp.minimum(bj - b_all, 0.0))
            t = jnp.where(valid, k32_ref[...] * e * qj, 0.0)
            for h in range(H_B):
                hs = slice(h * DK_B, (h + 1) * DK_B)
                a = jnp.sum(t[:, hs], axis=1, keepdims=True)
                row_j = jnp.sum(a * v32_ref[:, hs], axis=0, keepdims=True)
                oi_ref[:, hs] += jnp.where(is_j, row_j, 0.0)
            return carry

        lax.fori_loop(0, rows_blk, body, 0)
        o_ref[0] = oi_ref[...].astype(o_ref.dtype)


def _hgrn_dir(qs3, kf3, a3, lf3, reverse):
    bsz, s, _ = qs3.shape
    L = HGRN_ROWS
    nc = s // L
    w = D_MODEL
    cidx = (lambda c: nc - 1 - c) if reverse else (lambda c: c)
    d = 1 if reverse else 0
    return pl.pallas_call(
        functools.partial(_hgrn_kernel, reverse=reverse),
        grid=(bsz, nc),
        in_specs=[pl.BlockSpec((1, L, w), lambda b, c: (b, cidx(c), 0)),
                  pl.BlockSpec((1, L, w), lambda b, c: (b, cidx(c), d)),
                  pl.BlockSpec((1, L, w), lambda b, c: (b, cidx(c), 3)),
                  pl.BlockSpec((1, L, w), lambda b, c: (b, cidx(c), d))],
        out_specs=pl.BlockSpec((1, L, w), lambda b, c: (b, cidx(c), 0)),
        out_shape=jax.ShapeDtypeStruct((bsz, s, w), BF16),
        scratch_shapes=[pltpu.VMEM((H_B, DK_B, DK_B), F32)] + [pltpu.VMEM((L, w), F32)] * 5,
        compiler_params=_cparams("parallel", "arbitrary"),
        name="hgrn_bwd" if reverse else "hgrn_fwd",
    )(qs3, kf3, a3, lf3)


def _head_norm(v, n_heads, center):
    width = v.shape[1] // n_heads
    parts = []
    for h in range(n_heads):
        seg = v[:, h * width:(h + 1) * width]
        if center:
            seg = seg - jnp.mean(seg, axis=1, keepdims=True)
        parts.append(seg * lax.rsqrt(jnp.mean(seg * seg, axis=1, keepdims=True) + EPS))
    return jnp.concatenate(parts, axis=1)


def _mix_kernel(x_ref, haf_ref, hab_ref, hbf_ref, hbb_ref, og_ref, ga_ref, gb_ref, sgb_ref,
                na_ref, nb_ref, woa_ref, wob_ref, wo_ref, g_ref, b_ref, o_ref, ob_ref, *, alpha):
    h_a = og_ref[...].astype(F32) * (
        _head_norm(haf_ref[...].astype(F32) + hab_ref[...].astype(F32), H_A, True) * na_ref[...])
    y_a = _dot(h_a.astype(BF16), woa_ref[...])
    h_b = (_head_norm(hbf_ref[...].astype(F32) + hbb_ref[...].astype(F32), H_B, False) * nb_ref[...]
           ) * sgb_ref[...].astype(F32)
    y_b = _dot(h_b.astype(BF16), wob_ref[...])
    merged = ga_ref[...].astype(F32) * y_a + gb_ref[...].astype(F32) * y_b
    mix = _dot(merged.astype(BF16), wo_ref[...])
    y = _layer_norm_rows(alpha * x_ref[...] + mix, g_ref[...], b_ref[...])
    o_ref[...] = y
    ob_ref[...] = y.astype(BF16)


def _mix_call(x, haf, hab, hbf, hbb, sg, sl, norm_a, norm_b, woa, wob, wo, g, b, alpha):
    t, d = x.shape
    tm = min(256, t)
    tile = lambda j=0: pl.BlockSpec((tm, d), lambda i: (i, j))
    vec = lambda: pl.BlockSpec((1, d), lambda i: (0, 0))
    mat = lambda: pl.BlockSpec((d, d), lambda i: (0, 0))
    r = lambda a: a.reshape(1, d)
    return pl.pallas_call(
        functools.partial(_mix_kernel, alpha=alpha),
        grid=(t // tm,),
        in_specs=[tile(), tile(), tile(), tile(), tile(),
                  tile(0), tile(1), tile(2), tile(1),
                  vec(), vec(), mat(), mat(), mat(), vec(), vec()],
        out_specs=[tile(), tile()],
        out_shape=[jax.ShapeDtypeStruct((t, d), F32), jax.ShapeDtypeStruct((t, d), BF16)],
        compiler_params=_cparams("parallel"),
        name="mix_out_ln1",
    )(x, haf, hab, hbf, hbb, sg, sg, sg, sl, r(norm_a), r(norm_b), woa, wob, wo, r(g), r(b))


def _dense_kernel(xb_ref, p_ref, ws13_ref, ws2_ref, wple_ref, wpg_ref, o_ref):
    xb = xb_ref[...]
    h13 = _dot(xb, ws13_ref[...])
    hid = _silu(h13[:, :D_EXPERT]) * h13[:, D_EXPERT:]
    shared = _dot(hid.astype(BF16), ws2_ref[...])
    ple = _dot(p_ref[...].astype(BF16), wple_ref[...]) * jax.nn.sigmoid(_dot(xb, wpg_ref[...]))
    o_ref[...] = shared + ple


def _dense_call(xb, p, ws13, ws2, wple, wpg):
    t, d = xb.shape
    tm = min(512, t)
    full = lambda a: pl.BlockSpec(a.shape, lambda i: (0, 0))
    return pl.pallas_call(
        _dense_kernel,
        grid=(t // tm,),
        in_specs=[pl.BlockSpec((tm, d), lambda i: (i, 0)),
                  pl.BlockSpec((tm, PLE_DIM), lambda i: (i, 0)),
                  full(ws13), full(ws2), full(wple), full(wpg)],
        out_specs=pl.BlockSpec((tm, d), lambda i: (i, 0)),
        out_shape=jax.ShapeDtypeStruct((t, d), F32),
        compiler_params=_cparams("parallel"),
        name="shared_ple",
    )(xb, p, ws13, ws2, wple, wpg)


def _first_index(hit, iota, big):
    return jnp.min(jnp.where(hit, iota, big), axis=0, keepdims=True)


def _router_kernel(x_ref, wh_ref, wl_ref, bias_ref, idx_ref, wts_ref, pos_ref, cnt_ref, run_ref):
    @pl.when(pl.program_id(0) == 0)
    def _():
        run_ref[...] = jnp.zeros_like(run_ref)

    x = x_ref[...]
    xh = x.astype(BF16)
    xl = (x - xh.astype(F32)).astype(BF16)
    wh = wh_ref[...]
    logits = _dot_nt(wh, xh) + _dot_nt(wh, xl) + _dot_nt(wl_ref[...], xh)
    scores = jax.nn.sigmoid(logits)
    sel = scores + bias_ref[...]
    tm = sel.shape[1]
    ninf = -jnp.inf

    gi = lax.broadcasted_iota(jnp.int32, (GROUP_SIZE, tm), 0).astype(F32)
    g_rows = []
    for g in range(N_GROUPS):
        seg = sel[g * GROUP_SIZE:(g + 1) * GROUP_SIZE, :]
        m1 = jnp.max(seg, axis=0, keepdims=True)
        first = _first_index(seg == m1, gi, float(GROUP_SIZE))
        m2 = jnp.max(jnp.where(gi == first, ninf, seg), axis=0, keepdims=True)
        g_rows.append(m1 + m2)
    g_score = jnp.concatenate(g_rows, axis=0)

    ggi = lax.broadcasted_iota(jnp.int32, (N_GROUPS, tm), 0).astype(F32)
    g_keep = jnp.zeros((N_GROUPS, tm), F32)
    cur = g_score
    for _ in range(TOPK_GROUPS):
        mx = jnp.max(cur, axis=0, keepdims=True)
        hit = ggi == _first_index(cur == mx, ggi, float(N_GROUPS))
        g_keep = jnp.where(hit, 1.0, g_keep)
        cur = jnp.where(hit, ninf, cur)

    keep = jnp.concatenate(
        [jnp.broadcast_to(g_keep[g:g + 1, :], (GROUP_SIZE, tm)) for g in range(N_GROUPS)], axis=0)
    cur = jnp.where(keep > 0.0, sel, NEG)
    ei = lax.broadcasted_iota(jnp.int32, (N_EXPERTS, tm), 0).astype(F32)
    idx_rows, w_rows = [], []
    chosen = jnp.zeros((N_EXPERTS, tm), F32)
    for _ in range(TOP_K):
        mx = jnp.max(cur, axis=0, keepdims=True)
        first = _first_index(cur == mx, ei, float(N_EXPERTS))
        hit = ei == first
        idx_rows.append(first)
        w_rows.append(jnp.sum(jnp.where(hit, scores, 0.0), axis=0, keepdims=True))
        cur = jnp.where(hit, ninf, cur)
        chosen = jnp.where(hit, 1.0, chosen)
    w = jnp.concatenate(w_rows, axis=0)
    idx_ref[...] = jnp.concatenate(idx_rows, axis=0).astype(jnp.int32)
    wts_ref[...] = w / jnp.sum(w, axis=0, keepdims=True) * ROUTED_SCALE

    t_src = lax.broadcasted_iota(jnp.int32, (tm, tm), 0)
    t_dst = lax.broadcasted_iota(jnp.int32, (tm, tm), 1)
    before = (t_src < t_dst).astype(BF16)
    rank = _dot(chosen.astype(BF16), before) + run_ref[...]
    pos_ref[...] = jnp.concatenate(
        [jnp.sum(jnp.where(ei == r, rank, 0.0), axis=0, keepdims=True) for r in idx_rows],
        axis=0).astype(jnp.int32)
    run = run_ref[...] + jnp.sum(chosen, axis=1, keepdims=True)
    run_ref[...] = run
    cnt_ref[...] = run.astype(jnp.int32)


def _router_call(x, w_router, router_bias):
    t, d = x.shape
    tm = min(512, t)
    wt = w_router.T
    wh = wt.astype(BF16)
    wl = (wt - wh.astype(F32)).astype(BF16)
    kt = lambda: pl.BlockSpec((TOP_K, tm), lambda i: (0, i))
    return pl.pallas_call(
        _router_kernel,
        grid=(t // tm,),
        in_specs=[pl.BlockSpec((tm, d), lambda i: (i, 0)),
                  pl.BlockSpec((N_EXPERTS, d), lambda i: (0, 0)),
                  pl.BlockSpec((N_EXPERTS, d), lambda i: (0, 0)),
                  pl.BlockSpec((N_EXPERTS, 1), lambda i: (0, 0))],
        out_specs=[kt(), kt(), kt(), pl.BlockSpec((N_EXPERTS, 1), lambda i: (0, 0))],
        out_shape=[jax.ShapeDtypeStruct((TOP_K, t), jnp.int32),
                   jax.ShapeDtypeStruct((TOP_K, t), F32),
                   jax.ShapeDtypeStruct((TOP_K, t), jnp.int32),
                   jax.ShapeDtypeStruct((N_EXPERTS, 1), jnp.int32)],
        scratch_shapes=[pltpu.VMEM((N_EXPERTS, 1), F32)],
        compiler_params=_cparams("arbitrary"),
        name="router",
    )(x, wh, wl, router_bias.astype(F32).reshape(N_EXPERTS, 1))


def _dest_kernel(idx_ref, pos_ref, start_ref, dest_ref):
    tm = idx_ref.shape[1]
    ei = lax.broadcasted_iota(jnp.int32, (N_EXPERTS, tm), 0)
    start = start_ref[...].astype(F32)
    rows = [jnp.sum(jnp.where(ei == idx_ref[k:k + 1, :], start, 0.0), axis=0, keepdims=True)
            for k in range(TOP_K)]
    dest_ref[...] = pos_ref[...] + jnp.concatenate(rows, axis=0).astype(jnp.int32)


def _dest_call(idx_kt, pos_kt, p_start):
    t = idx_kt.shape[1]
    tm = min(1024, t)
    kt = lambda: pl.BlockSpec((TOP_K, tm), lambda i: (0, i))
    return pl.pallas_call(
        _dest_kernel,
        grid=(t // tm,),
        in_specs=[kt(), kt(), pl.BlockSpec((N_EXPERTS, 1), lambda i: (0, 0))],
        out_specs=kt(),
        out_shape=jax.ShapeDtypeStruct((TOP_K, t), jnp.int32),
        compiler_params=_cparams("parallel"),
        name="moe_dest",
    )(idx_kt, pos_kt, p_start.reshape(N_EXPERTS, 1))


LANES = 128
ROW_SUB = D_MODEL // (2 * LANES)
U32 = jnp.uint32
HIGH_HALF = 0xFFFF0000
DISPATCH_TILE = 256


def _row_tile(ref, r):
    return ref.at[pl.ds(pl.multiple_of(r * ROW_SUB, ROW_SUB), ROW_SUB), :]


def _lane_chunk(first_row, n_rows, j):
    return pl.ds(first_row * ROW_SUB + j, n_rows, stride=ROW_SUB)


def _bf16_bits(v):
    return lax.bitcast_convert_type(v.astype(BF16).astype(F32), U32)


def _to_row_tiles(dst_ref, v):
    for j in range(ROW_SUB):
        lo = _bf16_bits(v[:, (2 * j) * LANES:(2 * j + 1) * LANES]) >> 16
        hi = _bf16_bits(v[:, (2 * j + 1) * LANES:(2 * j + 2) * LANES])
        dst_ref[_lane_chunk(0, v.shape[0], j), :] = hi | lo


def _from_row_tiles(src_ref, first_row, n_rows):
    chunks = []
    for j in range(ROW_SUB):
        w = src_ref[_lane_chunk(first_row, n_rows, j), :]
        chunks.append(lax.bitcast_convert_type(w << 16, F32))
        chunks.append(lax.bitcast_convert_type(w & U32(HIGH_HALF), F32))
    return jnp.concatenate(chunks, axis=1)


def _dispatch_kernel(valid_ref, dest_ref, x_ref, xs_hbm, xt, zbuf, sem, zsem):
    i = pl.program_id(0)
    tm = xt.shape[1] // ROW_SUB
    bm = zbuf.shape[0] // ROW_SUB
    n_blk = valid_ref.shape[0]
    slot = i % 2

    @pl.when(pl.program_id(0) == 0)
    def _():
        zbuf[...] = jnp.zeros_like(zbuf)

        def zero_copy(i):
            first = pl.multiple_of(i * (bm * ROW_SUB), bm * ROW_SUB)
            return pltpu.make_async_copy(zbuf, xs_hbm.at[pl.ds(first, bm * ROW_SUB), :], zsem)

        def start(i, carry):
            @pl.when(valid_ref[i] < bm)
            def _():
                zero_copy(i).start()
            return carry

        def wait(i, carry):
            @pl.when(valid_ref[i] < bm)
            def _():
                zero_copy(i).wait()
            return carry

        lax.fori_loop(0, n_blk, start, 0)
        lax.fori_loop(0, n_blk, wait, 0)

    stage = xt.at[slot]
    _to_row_tiles(stage, x_ref[...].astype(F32))

    def body(r, carry):
        for k in range(TOP_K):
            pltpu.make_async_copy(_row_tile(stage, r), _row_tile(xs_hbm, dest_ref[k, r]), sem.at[slot]).start()
        return carry

    lax.fori_loop(0, tm, body, 0)

    def drain(s):
        for k in range(TOP_K):
            pltpu.make_async_copy(xt.at[s], xs_hbm.at[pl.ds(0, tm * ROW_SUB), :], sem.at[s]).wait()

    @pl.when(i > 0)
    def _():
        drain(1 - slot)

    @pl.when(i == pl.num_programs(0) - 1)
    def _():
        drain(slot)


def _dispatch_call(x, dest_kt, blk_valid):
    t, d = x.shape
    tm = min(DISPATCH_TILE, t)
    bm = MOE_BLOCK
    n_blk = blk_valid.shape[0]
    grid_spec = pltpu.PrefetchScalarGridSpec(
        num_scalar_prefetch=1,
        grid=(t // tm,),
        in_specs=[pl.BlockSpec((TOP_K, tm), lambda i, v: (0, i), memory_space=pltpu.SMEM),
                  pl.BlockSpec((tm, d), lambda i, v: (i, 0))],
        out_specs=pl.BlockSpec(memory_space=pl.ANY),
        scratch_shapes=[pltpu.VMEM((2, tm * ROW_SUB, LANES), U32), pltpu.VMEM((bm * ROW_SUB, LANES), U32),
                        pltpu.SemaphoreType.DMA((2,)), pltpu.SemaphoreType.DMA(())],
    )
    return pl.pallas_call(
        _dispatch_kernel,
        grid_spec=grid_spec,
        out_shape=jax.ShapeDtypeStruct((n_blk * bm * ROW_SUB, LANES), U32),
        compiler_params=_cparams("arbitrary"),
        name="moe_dispatch",
    )(blk_valid, dest_kt, x)


def _expert_kernel(blk_e_ref, nused_ref, xs_ref, w13_ref, w2_ref, o_ref):
    i = pl.program_id(0)

    @pl.when(i < nused_ref[0])
    def _():
        xb = _from_row_tiles(xs_ref, 0, xs_ref.shape[0] // ROW_SUB).astype(BF16)
        h13 = _dot(xb, w13_ref[0])
        hid = _silu(h13[:, :D_EXPERT]) * h13[:, D_EXPERT:]
        _to_row_tiles(o_ref, _dot(hid.astype(BF16), w2_ref[0]))

    @pl.when(i >= nused_ref[0])
    def _():
        o_ref[...] = jnp.zeros_like(o_ref)


def _expert_call(xs, blk_e, n_used, w13, w2):
    bm = MOE_BLOCK
    n_blk = blk_e.shape[0]
    d = D_MODEL
    xs_idx = lambda i, e, n: (jnp.minimum(i, jnp.maximum(n[0] - 1, 0)), 0)
    grid_spec = pltpu.PrefetchScalarGridSpec(
        num_scalar_prefetch=2,
        grid=(n_blk,),
        in_specs=[pl.BlockSpec((bm * ROW_SUB, LANES), xs_idx),
                  pl.BlockSpec((1, d, 2 * D_EXPERT), lambda i, e, n: (e[i], 0, 0)),
                  pl.BlockSpec((1, D_EXPERT, d), lambda i, e, n: (e[i], 0, 0))],
        out_specs=pl.BlockSpec((bm * ROW_SUB, LANES), lambda i, e, n: (i, 0)),
    )
    return pl.pallas_call(
        _expert_kernel,
        grid_spec=grid_spec,
        out_shape=jax.ShapeDtypeStruct((n_blk * bm * ROW_SUB, LANES), U32),
        compiler_params=_cparams("arbitrary"),
        name="routed_experts",
    )(blk_e, n_used, xs, w13, w2)


def _combine_kernel(dest_ref, destn_ref, ys_hbm, w_ref, dense_ref, x_ref, g_ref, b_ref,
                    o_ref, ob_ref, gbuf, sem, *, alpha):
    i = pl.program_id(0)
    n = pl.num_programs(0)
    rows = gbuf.shape[1] // ROW_SUB
    tm = rows // TOP_K
    cur = i % 2

    def issue(d_ref, s):
        def body(r, carry):
            for k in range(TOP_K):
                pltpu.make_async_copy(_row_tile(ys_hbm, d_ref[k, r]), _row_tile(gbuf.at[s], k * tm + r),
                                      sem.at[s]).start()
            return carry
        lax.fori_loop(0, tm, body, 0)

    @pl.when(i == 0)
    def _():
        issue(dest_ref, 0)

    @pl.when(i + 1 < n)
    def _():
        issue(destn_ref, 1 - cur)

    pltpu.make_async_copy(ys_hbm.at[pl.ds(0, rows * ROW_SUB), :], gbuf.at[cur], sem.at[cur]).wait()
    w = w_ref[...]
    routed = _from_row_tiles(gbuf.at[cur], 0, tm) * w[:, 0:1]
    for k in range(1, TOP_K):
        routed = routed + _from_row_tiles(gbuf.at[cur], k * tm, tm) * w[:, k:k + 1]
    y = _layer_norm_rows(alpha * x_ref[...] + routed + dense_ref[...], g_ref[...], b_ref[...])
    o_ref[...] = y
    ob_ref[...] = y.astype(BF16)


def _combine_call(ys, dest_kt, wts, dense, x, g, b, alpha):
    t, d = x.shape
    tm = min(COMBINE_TILE, t)
    nt = t // tm
    rows = tm * TOP_K
    tile = lambda: pl.BlockSpec((tm, d), lambda i: (i, 0))
    vec = lambda: pl.BlockSpec((1, d), lambda i: (0, 0))
    return pl.pallas_call(
        functools.partial(_combine_kernel, alpha=alpha),
        grid=(nt,),
        in_specs=[pl.BlockSpec((TOP_K, tm), lambda i: (0, i), memory_space=pltpu.SMEM),
                  pl.BlockSpec((TOP_K, tm), lambda i: (0, jnp.minimum(i + 1, nt - 1)),
                               memory_space=pltpu.SMEM),
                  pl.BlockSpec(memory_space=pl.ANY),
                  pl.BlockSpec((tm, TOP_K), lambda i: (i, 0)),
                  tile(), tile(), vec(), vec()],
        out_specs=[tile(), tile()],
        out_shape=[jax.ShapeDtypeStruct((t, d), F32), jax.ShapeDtypeStruct((t, d), BF16)],
        scratch_shapes=[pltpu.VMEM((2, rows * ROW_SUB, LANES), U32), pltpu.SemaphoreType.DMA((2,))],
        compiler_params=_cparams("arbitrary"),
        name="combine_ln2",
    )(dest_kt, dest_kt, ys, wts, dense, x, g.reshape(1, d), b.reshape(1, d))


def _block_tables(counts, n_blk):
    bm = MOE_BLOCK
    padded = (counts + bm - 1) // bm * bm
    p_end = jnp.cumsum(padded)
    p_start = p_end - padded
    blk_row = jnp.arange(n_blk, dtype=jnp.int32) * bm
    blk_e = jnp.minimum(jnp.sum(p_end[None, :] <= blk_row[:, None], axis=1), N_EXPERTS - 1).astype(jnp.int32)
    blk_valid = jnp.clip(counts[blk_e] - (blk_row - p_start[blk_e]), 0, bm).astype(jnp.int32)
    n_used = (p_end[-1] // bm).astype(jnp.int32).reshape(1)
    return p_start.astype(jnp.int32), blk_e, blk_valid, n_used


def _prep_layer(li, lower, w_in, b_in, conv_w, conv_b, norm_a, norm_b, w_out_a, w_out_b, w_o,
                ln1_g, ln1_b, w_ple, w_ple_gate, w_router, router_bias, w1, w3, w2, ws1, ws3, ws2,
                ln2_g, ln2_b):
    d = D_MODEL
    w = w_in[li]
    bi = b_in[li]
    seg = lambda a, lo, hi: a[..., lo:hi]
    o_qkv, o_oa, o_g = 0, 3 * d, 4 * d
    o_qb = 4 * d + 4 * H_A
    o_ff, o_ib, o_gb, o_ga = o_qb + d, o_qb + 3 * d, o_qb + 4 * d, o_qb + 5 * d
    cat = lambda parts: jnp.concatenate(parts, axis=-1)
    raw_cols = [(o_qkv, o_qkv + 3 * d), (o_ib, o_ib + d)]
    sig_cols = [(o_oa, o_oa + d), (o_ga, o_ga + 2 * d)]
    silu_cols = [(o_qb, o_qb + d), (o_gb, o_gb + d)]
    return dict(
        w_raw=cat([seg(w, *c) for c in raw_cols]).astype(BF16), b_raw=cat([seg(bi, *c) for c in raw_cols]),
        w_sig=cat([seg(w, *c) for c in sig_cols]).astype(BF16), b_sig=cat([seg(bi, *c) for c in sig_cols]),
        w_silu=cat([seg(w, *c) for c in silu_cols]).astype(BF16), b_silu=cat([seg(bi, *c) for c in silu_cols]),
        w_fg=seg(w, o_ff, o_ff + 2 * d).astype(BF16), b_fg=seg(bi, o_ff, o_ff + 2 * d),
        w_g=seg(w, o_g, o_g + 4 * H_A).astype(BF16), b_g=seg(bi, o_g, o_g + 4 * H_A),
        lower=lower[li].reshape(-1),
        conv_w=conv_w[li], conv_b=conv_b[li], norm_a=norm_a[li], norm_b=norm_b[li],
        w_out_a=w_out_a[li].astype(BF16), w_out_b=w_out_b[li].astype(BF16), w_o=w_o[li].astype(BF16),
        ln1_g=ln1_g[li], ln1_b=ln1_b[li],
        w_ple=w_ple[li].astype(BF16), w_ple_gate=w_ple_gate[li].astype(BF16),
        w_router=w_router[li], router_bias=router_bias[li],
        w13=jnp.concatenate([w1[li], w3[li]], axis=-1).astype(BF16), w2=w2[li].astype(BF16),
        ws13=jnp.concatenate([ws1[li], ws3[li]], axis=-1).astype(BF16), ws2=ws2[li].astype(BF16),
        ln2_g=ln2_g[li], ln2_b=ln2_b[li],
    )


def _layer(x, xb, p, lp, bsz, s, alpha):
    t, d = x.shape
    a = _proj_act(xb, lp["w_raw"], lp["b_raw"], "raw", "proj_raw")
    sg = _proj_act(xb, lp["w_sig"], lp["b_sig"], "sigmoid", "proj_sigmoid")
    sl = _proj_act(xb, lp["w_silu"], lp["b_silu"], "silu", "proj_silu")
    kf, lf = _proj_forget(xb, lp["w_fg"], lp["b_fg"], lp["lower"])
    gc, gr = _proj_gates(xb, lp["w_g"], lp["b_g"])

    a3 = a.reshape(bsz, s, 4 * d)
    qk3 = _conv_silu(a3, lp["conv_w"], lp["conv_b"])
    gc3 = gc.reshape(bsz, s, 4 * H_A)
    haf = _mlstm_dir(qk3, a3, gc3, gr, False).reshape(t, d)
    hab = _mlstm_dir(qk3, a3, gc3, gr, True).reshape(t, d)

    sl3 = sl.reshape(bsz, s, 2 * d)
    kf3 = kf.reshape(bsz, s, 2 * d)
    lf3 = lf.reshape(bsz, s, 2 * d)
    hbf = _hgrn_dir(sl3, kf3, a3, lf3, False).reshape(t, d)
    hbb = _hgrn_dir(sl3, kf3, a3, lf3, True).reshape(t, d)

    x1, x1b = _mix_call(x, haf, hab, hbf, hbb, sg, sl, lp["norm_a"], lp["norm_b"],
                        lp["w_out_a"], lp["w_out_b"], lp["w_o"], lp["ln1_g"], lp["ln1_b"], alpha)

    dense = _dense_call(x1b, p, lp["ws13"], lp["ws2"], lp["w_ple"], lp["w_ple_gate"])
    idx_kt, wts_kt, pos_kt, counts = _router_call(x1, lp["w_router"], lp["router_bias"])
    n_blk = t * TOP_K // MOE_BLOCK + N_EXPERTS
    p_start, blk_e, blk_valid, n_used = _block_tables(counts.reshape(-1), n_blk)
    dest_kt = _dest_call(idx_kt, pos_kt, p_start)
    xs = _dispatch_call(x1b, dest_kt, blk_valid)
    ys = _expert_call(xs, blk_e, n_used, lp["w13"], lp["w2"])
    return _combine_call(ys, dest_kt, wts_kt.T, dense, x1, lp["ln2_g"], lp["ln2_b"], alpha)


def kernel(x_prompt, x_sample, p_prompt, p_sample, emb_ln_g, emb_ln_b, w_in, b_in, conv_w, conv_b, norm_a, norm_b, lb_param, w_out_a, w_out_b, w_o, ln1_g, ln1_b, w_ple, w_ple_gate, w_router, router_bias, w1, w3, w2, ws1, ws3, ws2, ln2_g, ln2_b):
    depth = w_in.shape[0]
    alpha = (2.0 * depth) ** 0.25
    lb_soft = jax.nn.softmax(lb_param.astype(F32), axis=0)
    lower = jnp.cumsum(lb_soft, axis=0) - lb_soft[0:1]
    layers = [_prep_layer(li, lower, w_in, b_in, conv_w, conv_b, norm_a, norm_b, w_out_a, w_out_b, w_o,
                          ln1_g, ln1_b, w_ple, w_ple_gate, w_router, router_bias, w1, w3, w2,
                          ws1, ws3, ws2, ln2_g, ln2_b) for li in range(depth)]

    def trunk(x3, p4):
        bsz, s, d = x3.shape
        x, xb = _layer_norm_call(x3.reshape(bsz * s, d), emb_ln_g, emb_ln_b)
        for li in range(depth):
            x, xb = _layer(x, xb, p4[li].reshape(bsz * s, -1), layers[li], bsz, s, alpha)
        return x.reshape(bsz, s, d)

    return (trunk(x_prompt, p_prompt), trunk(x_sample, p_sample))
```

```python
import functools

import jax
import jax.numpy as jnp
from jax import lax
from jax.experimental import pallas as pl
from jax.experimental.pallas import tpu as pltpu

F32 = jnp.float32
BF16 = jnp.bfloat16

D_MODEL = 1024
PLE_DIM = 256
H_A = 4
DH_A = D_MODEL // H_A
CHUNK_A = 128
H_B = 8
DK_B = 128
CHUNK_B = 128
N_EXPERTS = 256
TOP_K = 8
N_GROUPS = 8
GROUP_SIZE = N_EXPERTS // N_GROUPS
TOPK_GROUPS = 4
D_EXPERT = 256
ROUTED_SCALE = 2.5
EPS = 1e-5
NEG = -1e30
TINY = 1e-30

HGRN_SAFE_RANGE = 80.0
HGRN_ROWS = CHUNK_B
MOE_BLOCK = 512
COMBINE_TILE = 128
VMEM_LIMIT = 56 * 1024 * 1024


def _cparams(*sem):
    return pltpu.CompilerParams(dimension_semantics=sem, vmem_limit_bytes=VMEM_LIMIT)


def _split3(x):
    hi = x.astype(BF16)
    r1 = x - hi.astype(F32)
    mid = r1.astype(BF16)
    lo = (r1 - mid.astype(F32)).astype(BF16)
    return hi, mid, lo


def _dot(a, b):
    return jnp.dot(a, b, preferred_element_type=F32)


def _dot_nt(a, b):
    return lax.dot_general(a, b, (((1,), (1,)), ((), ())), preferred_element_type=F32)


def _dot_tn(a, b):
    return lax.dot_general(a, b, (((0,), (0,)), ((), ())), preferred_element_type=F32)


def _layer_norm_rows(v, g, b):
    mu = jnp.mean(v, axis=-1, keepdims=True)
    vc = v - mu
    var = jnp.mean(vc * vc, axis=-1, keepdims=True)
    return vc * lax.rsqrt(var + EPS) * g + b


def _log_sigmoid(z):
    return jnp.minimum(z, 0.0) - jnp.log1p(jnp.exp(-jnp.abs(z)))


def _silu(z):
    return z * jax.nn.sigmoid(z)


def _ln_kernel(x_ref, g_ref, b_ref, o_ref, ob_ref):
    y = _layer_norm_rows(x_ref[...], g_ref[...], b_ref[...])
    o_ref[...] = y
    ob_ref[...] = y.astype(BF16)


def _layer_norm_call(x, g, b):
    t, d = x.shape
    tm = min(512, t)
    return pl.pallas_call(
        _ln_kernel,
        grid=(t // tm,),
        in_specs=[pl.BlockSpec((tm, d), lambda i: (i, 0)),
                  pl.BlockSpec((1, d), lambda i: (0, 0)),
                  pl.BlockSpec((1, d), lambda i: (0, 0))],
        out_specs=[pl.BlockSpec((tm, d), lambda i: (i, 0)),
                   pl.BlockSpec((tm, d), lambda i: (i, 0))],
        out_shape=[jax.ShapeDtypeStruct((t, d), F32), jax.ShapeDtypeStruct((t, d), BF16)],
        compiler_params=_cparams("parallel"),
        name="emb_layer_norm",
    )(x, g.reshape(1, d), b.reshape(1, d))


PROJ_SUB = 256


def _row_blocks(n):
    step = min(PROJ_SUB, n)
    return [slice(r, r + step) for r in range(0, n, step)]


def _proj_act_kernel(x_ref, w_ref, b_ref, o_ref, *, act):
    for rs in _row_blocks(x_ref.shape[0]):
        z = _dot(x_ref[rs, :], w_ref[...]) + b_ref[...]
        if act == "sigmoid":
            z = jax.nn.sigmoid(z)
        elif act == "silu":
            z = _silu(z)
        o_ref[rs, :] = z.astype(o_ref.dtype)


def _proj_act(xb, w, b, act, name):
    t, d = xb.shape
    n = w.shape[1]
    tm = min(1024, t)
    tn = 1024
    return pl.pallas_call(
        functools.partial(_proj_act_kernel, act=act),
        grid=(t // tm, n // tn),
        in_specs=[pl.BlockSpec((tm, d), lambda i, j: (i, 0)),
                  pl.BlockSpec((d, tn), lambda i, j: (0, j)),
                  pl.BlockSpec((1, tn), lambda i, j: (0, j))],
        out_specs=pl.BlockSpec((tm, tn), lambda i, j: (i, j)),
        out_shape=jax.ShapeDtypeStruct((t, n), BF16),
        compiler_params=_cparams("parallel", "arbitrary"),
        name=name,
    )(xb, w, b.reshape(1, n))


def _proj_forget_kernel(x_ref, w_ref, b_ref, oml_ref, la_ref, lc_ref, k_ref, lf_ref):
    a = la_ref[...]
    for rs in _row_blocks(x_ref.shape[0]):
        z = _dot(x_ref[rs, :], w_ref[...]) + b_ref[...]
        e = jnp.exp(-jnp.abs(z))
        r = 1.0 / (1.0 + e)
        k_ref[rs, :] = (oml_ref[...] * (jnp.where(z > 0.0, e, 1.0) * r)).astype(BF16)
        u = lc_ref[...] + (jnp.minimum(z, 0.0) + jnp.log(r))
        lf_ref[rs, :] = jnp.maximum(a, u) + jnp.log(1.0 + jnp.exp(-jnp.abs(a - u)))


def _proj_forget(xb, w, b, lower):
    t, d = xb.shape
    n = w.shape[1]
    tm = min(1024, t)
    tn = 1024
    lb = lower.reshape(1, n)
    vec = lambda: pl.BlockSpec((1, tn), lambda i, j: (0, j))
    return pl.pallas_call(
        _proj_forget_kernel,
        grid=(t // tm, n // tn),
        in_specs=[pl.BlockSpec((tm, d), lambda i, j: (i, 0)),
                  pl.BlockSpec((d, tn), lambda i, j: (0, j)),
                  vec(), vec(), vec(), vec()],
        out_specs=[pl.BlockSpec((tm, tn), lambda i, j: (i, j)),
                   pl.BlockSpec((tm, tn), lambda i, j: (i, j))],
        out_shape=[jax.ShapeDtypeStruct((t, n), BF16), jax.ShapeDtypeStruct((t, n), F32)],
        compiler_params=_cparams("parallel", "arbitrary"),
        name="proj_forget",
    )(xb, w, b.reshape(1, n), 1.0 - lb, jnp.log(jnp.maximum(lb, TINY)), jnp.log1p(-lb))


def _proj_gates_kernel(x_ref, w_ref, wt_ref, b_ref, bt_ref, gc_ref, gr_ref):
    x = x_ref[...]
    zc = _dot(x, w_ref[...]) + b_ref[...]
    lane = lax.broadcasted_iota(jnp.int32, zc.shape, 1)
    gc_ref[...] = jnp.where(lane >= 2 * H_A, _log_sigmoid(zc), zc)
    zr = _dot_nt(wt_ref[...], x) + bt_ref[...]
    sub = lax.broadcasted_iota(jnp.int32, zr.shape, 0)
    gr_ref[...] = jnp.where(sub >= 2 * H_A, _log_sigmoid(zr), zr)


def _proj_gates(xb, w, b):
    t, d = xb.shape
    n = w.shape[1]
    tm = min(1024, t)
    return pl.pallas_call(
        _proj_gates_kernel,
        grid=(t // tm,),
        in_specs=[pl.BlockSpec((tm, d), lambda i: (i, 0)),
                  pl.BlockSpec((d, n), lambda i: (0, 0)),
                  pl.BlockSpec((n, d), lambda i: (0, 0)),
                  pl.BlockSpec((1, n), lambda i: (0, 0)),
                  pl.BlockSpec((n, 1), lambda i: (0, 0))],
        out_specs=[pl.BlockSpec((tm, n), lambda i: (i, 0)),
                   pl.BlockSpec((n, tm), lambda i: (0, i))],
        out_shape=[jax.ShapeDtypeStruct((t, n), F32), jax.ShapeDtypeStruct((n, t), F32)],
        compiler_params=_cparams("parallel"),
        name="proj_gates",
    )(xb, w, w.T, b.reshape(1, n), b.reshape(n, 1))


def _conv_kernel(x_ref, w_ref, b_ref, o_ref, *, rows):
    s = x_ref.shape[1]
    n_rc = s // rows
    w = w_ref[...]
    w0, w1, w2 = w[0:1, :], w[1:2, :], w[2:3, :]
    bias = b_ref[...]
    ridx = lax.broadcasted_iota(jnp.int32, (rows, x_ref.shape[2]), 0)

    def body(r, carry):
        base = pl.multiple_of(r * rows, rows)
        cur = x_ref[0, pl.ds(base, rows), :].astype(F32)
        pstart = pl.multiple_of(jnp.maximum(base - 16, 0), 16)
        nstart = pl.multiple_of(jnp.minimum(base + rows, s - 16), 16)
        prev_row = x_ref[0, pl.ds(pstart, 16), :][15:16, :].astype(F32)
        next_row = x_ref[0, pl.ds(nstart, 16), :][0:1, :].astype(F32)
        prev_row = jnp.where(r > 0, prev_row, 0.0)
        next_row = jnp.where(r < n_rc - 1, next_row, 0.0)
        up = jnp.where(ridx == 0, prev_row, pltpu.roll(cur, 1, 0))
        dn = jnp.where(ridx == rows - 1, next_row, pltpu.roll(cur, rows - 1, 0))
        z = w0 * up + w1 * cur + w2 * dn + bias
        o_ref[0, pl.ds(base, rows), :] = _silu(z).astype(o_ref.dtype)
        return carry

    lax.fori_loop(0, n_rc, body, 0)


def _conv_silu(a3, conv_w, conv_b):
    bsz, s, _ = a3.shape
    c = conv_w.shape[1]
    cw = 256
    rows = min(256, s)
    return pl.pallas_call(
        functools.partial(_conv_kernel, rows=rows),
        grid=(bsz, c // cw),
        in_specs=[pl.BlockSpec((1, s, cw), lambda b, j: (b, 0, j)),
                  pl.BlockSpec((3, cw), lambda b, j: (0, j)),
                  pl.BlockSpec((1, cw), lambda b, j: (0, j))],
        out_specs=pl.BlockSpec((1, s, cw), lambda b, j: (b, 0, j)),
        out_shape=jax.ShapeDtypeStruct((bsz, s, c), BF16),
        compiler_params=_cparams("parallel", "parallel"),
        name="conv_silu",
    )(a3, conv_w, conv_b.reshape(1, c))


def _mlstm_kernel(q_ref, k_ref, v_ref, gc_ref, gr_ref, o_ref, c_ref, n_ref, m_ref, *, reverse):
    L = CHUNK_A

    @pl.when(pl.program_id(1) == 0)
    def _():
        c_ref[...] = jnp.zeros_like(c_ref)
        n_ref[...] = jnp.zeros_like(n_ref)
        m_ref[...] = jnp.zeros_like(m_ref)

    row = lax.broadcasted_iota(jnp.int32, (L, L), 0)
    col = lax.broadcasted_iota(jnp.int32, (L, L), 1)
    if reverse:
        mask = col >= row
        mask_t = row >= col
        last = 0
    else:
        mask = col <= row
        mask_t = row <= col
        last = L - 1
    m_mat = mask.astype(BF16)
    mt_mat = mask_t.astype(BF16)

    gc = gc_ref[0]
    gr = gr_ref[...]
    cum_c = sum(_dot(m_mat, p) for p in _split3(gc))
    cum_r = sum(_dot(p, mt_mat) for p in _split3(gr))
    off = H_A if reverse else 0
    scale = DH_A ** -0.5

    for h in range(H_A):
        ii = off + h
        fi = 2 * H_A + off + h
        hs = slice(h * DH_A, (h + 1) * DH_A)
        bc = cum_c[:, fi:fi + 1]
        br = cum_r[fi:fi + 1, :]
        ic = gc[:, ii:ii + 1]
        ir = gr[ii:ii + 1, :]
        m_st = m_ref[h:h + 1, 0:1]
        qh = q_ref[0, :, hs]
        kh = k_ref[0, :, hs]
        vh = v_ref[0, :, hs]
        c_st = c_ref[h]
        n_st = n_ref[h:h + 1, :]

        d_mat = jnp.where(mask, bc - br + ir, NEG)
        m_inter = bc + m_st
        m_row = jnp.maximum(m_inter, jnp.max(d_mat, axis=1, keepdims=True))
        w_inter = jnp.exp(m_inter - m_row)
        s_mat = _dot_nt(qh, kh) * scale * jnp.exp(d_mat - m_row)
        num = w_inter * _dot(qh, c_st.astype(BF16)) + _dot(s_mat.astype(BF16), vh)
        qn = jnp.sum(qh.astype(F32) * n_st, axis=1, keepdims=True)
        den = w_inter * qn + jnp.sum(s_mat, axis=1, keepdims=True)
        inv = 1.0 / jnp.maximum(jnp.abs(den), jnp.exp(-m_row))
        o_ref[0, :, hs] = (num * inv).astype(o_ref.dtype)

        b_last = bc[last:last + 1, :]
        g_end_r = b_last - br + ir
        g_end_c = b_last - bc + ic
        m_next = jnp.maximum(b_last + m_st, jnp.max(g_end_r, axis=1, keepdims=True))
        decay = jnp.exp(b_last + m_st - m_next)
        wk = jnp.exp(g_end_c - m_next) * (kh.astype(F32) * scale)
        c_ref[h] = decay * c_st + _dot_tn(wk.astype(BF16), vh)
        n_ref[h:h + 1, :] = decay * n_st + jnp.sum(wk, axis=0, keepdims=True)
        m_ref[h:h + 1, :] = jnp.broadcast_to(m_next, (1, m_ref.shape[1]))


def _mlstm_dir(qk3, a3, gc3, gr, reverse):
    bsz, s, _ = qk3.shape
    L = CHUNK_A
    nc = s // L
    cidx = (lambda c: nc - 1 - c) if reverse else (lambda c: c)
    return pl.pallas_call(
        functools.partial(_mlstm_kernel, reverse=reverse),
        grid=(bsz, nc),
        in_specs=[pl.BlockSpec((1, L, D_MODEL), lambda b, c: (b, cidx(c), 0)),
                  pl.BlockSpec((1, L, D_MODEL), lambda b, c: (b, cidx(c), 1)),
                  pl.BlockSpec((1, L, D_MODEL), lambda b, c: (b, cidx(c), 2)),
                  pl.BlockSpec((1, L, 4 * H_A), lambda b, c: (b, cidx(c), 0)),
                  pl.BlockSpec((4 * H_A, L), lambda b, c: (0, b * nc + cidx(c)))],
        out_specs=pl.BlockSpec((1, L, D_MODEL), lambda b, c: (b, cidx(c), 0)),
        out_shape=jax.ShapeDtypeStruct((bsz, s, D_MODEL), BF16),
        scratch_shapes=[pltpu.VMEM((H_A, DH_A, DH_A), F32),
                        pltpu.VMEM((H_A, DH_A), F32),
                        pltpu.VMEM((H_A, 128), F32)],
        compiler_params=_cparams("parallel", "arbitrary"),
        name="mlstm_bwd" if reverse else "mlstm_fwd",
    )(qk3, qk3, a3, gc3, gr)


def _hgrn_kernel(q_ref, k_ref, v_ref, lf_ref, o_ref, st_ref, oi_ref, b_ref, q32_ref, k32_ref, v32_ref, *,
                 reverse):
    L = CHUNK_B
    rows_blk = q_ref.shape[1]
    n_sub = rows_blk // L
    width = q_ref.shape[2]

    @pl.when(pl.program_id(1) == 0)
    def _():
        st_ref[...] = jnp.zeros_like(st_ref)

    row = lax.broadcasted_iota(jnp.int32, (rows_blk, rows_blk), 0)
    col = lax.broadcasted_iota(jnp.int32, (rows_blk, rows_blk), 1)
    vis = (col >= row) if reverse else (col <= row)
    same = (row // L) == (col // L)
    last = 0 if reverse else L - 1
    mask = vis[0:L, 0:L]

    m_all = jnp.logical_and(vis, same).astype(BF16)
    b = sum(_dot(m_all, p) for p in _split3(lf_ref[0]))

    def per_chunk_row(r):
        return jnp.concatenate(
            [jnp.broadcast_to(b[s * L + r:s * L + r + 1, :], (L, width)) for s in range(n_sub)], axis=0)

    q = q_ref[0].astype(F32)
    k = k_ref[0].astype(F32)
    v = v_ref[0]
    q_in = (q * jnp.exp(b)).astype(BF16)
    k_out = (k * jnp.exp(per_chunk_row(last) - b)).astype(BF16)
    dev = b - per_chunk_row(L // 2)
    rng = jnp.max(jnp.abs(dev))
    qt = (q * jnp.exp(dev)).astype(BF16)
    kt = (k * jnp.exp(-dev)).astype(BF16)

    for s in (reversed(range(n_sub)) if reverse else range(n_sub)):
        rs = slice(s * L, (s + 1) * L)
        dec = jnp.exp(b[s * L + last:s * L + last + 1, :])
        for h in range(H_B):
            hs = slice(h * DK_B, (h + 1) * DK_B)
            st = st_ref[h]
            o_inter = _dot_nt(q_in[rs, hs], st.astype(BF16))
            a = jnp.where(mask, _dot_nt(qt[rs, hs], kt[rs, hs]), 0.0)
            o_ref[0, rs, hs] = (o_inter + _dot(a.astype(BF16), v[rs, hs])).astype(o_ref.dtype)
            oi_ref[rs, hs] = o_inter
            st_ref[h] = dec[:, hs] * st + _dot_tn(v[rs, hs], k_out[rs, hs])

    @pl.when(rng > HGRN_SAFE_RANGE)
    def _():
        b_ref[...] = b
        q32_ref[...] = q
        k32_ref[...] = k
        v32_ref[...] = v.astype(F32)
        sidx = lax.broadcasted_iota(jnp.int32, (rows_blk, 1), 0)

        def body(j, carry):
            is_j = sidx == j
            b_all = b_ref[...]
            bj = jnp.sum(jnp.where(is_j, b_all, 0.0), axis=0, keepdims=True)
            qj = jnp.sum(jnp.where(is_j, q32_ref[...], 0.0), axis=0, keepdims=True)
            valid = jnp.logical_and((sidx >= j) if reverse else (sidx <= j), sidx // L == j // L)
            e = jnp.exp(jnp.minimum(bj - b_all, 0.0))
            t = jnp.where(valid, k32_ref[...] * e * qj, 0.0)
            for h in range(H_B):
                hs = slice(h * DK_B, (h + 1) * DK_B)
                a = jnp.sum(t[:, hs], axis=1, keepdims=True)
                row_j = jnp.sum(a * v32_ref[:, hs], axis=0, keepdims=True)
                oi_ref[:, hs] += jnp.where(is_j, row_j, 0.0)
            return carry

        lax.fori_loop(0, rows_blk, body, 0)
        o_ref[0] = oi_ref[...].astype(o_ref.dtype)


def _hgrn_dir(qs3, kf3, a3, lf3, reverse):
    bsz, s, _ = qs3.shape
    L = HGRN_ROWS
    nc = s // L
    w = D_MODEL
    cidx = (lambda c: nc - 1 - c) if reverse else (lambda c: c)
    d = 1 if reverse else 0
    return pl.pallas_call(
        functools.partial(_hgrn_kernel, reverse=reverse),
        grid=(bsz, nc),
        in_specs=[pl.BlockSpec((1, L, w), lambda b, c: (b, cidx(c), 0)),
                  pl.BlockSpec((1, L, w), lambda b, c: (b, cidx(c), d)),
                  pl.BlockSpec((1, L, w), lambda b, c: (b, cidx(c), 3)),
                  pl.BlockSpec((1, L, w), lambda b, c: (b, cidx(c), d))],
        out_specs=pl.BlockSpec((1, L, w), lambda b, c: (b, cidx(c), 0)),
        out_shape=jax.ShapeDtypeStruct((bsz, s, w), BF16),
        scratch_shapes=[pltpu.VMEM((H_B, DK_B, DK_B), F32)] + [pltpu.VMEM((L, w), F32)] * 5,
        compiler_params=_cparams("parallel", "arbitrary"),
        name="hgrn_bwd" if reverse else "hgrn_fwd",
    )(qs3, kf3, a3, lf3)


def _head_norm(v, n_heads, center):
    width = v.shape[1] // n_heads
    parts = []
    for h in range(n_heads):
        seg = v[:, h * width:(h + 1) * width]
        if center:
            seg = seg - jnp.mean(seg, axis=1, keepdims=True)
        parts.append(seg * lax.rsqrt(jnp.mean(seg * seg, axis=1, keepdims=True) + EPS))
    return jnp.concatenate(parts, axis=1)


def _mix_kernel(x_ref, haf_ref, hab_ref, hbf_ref, hbb_ref, og_ref, ga_ref, gb_ref, sgb_ref,
                na_ref, nb_ref, woa_ref, wob_ref, wo_ref, g_ref, b_ref, o_ref, ob_ref, *, alpha):
    h_a = og_ref[...].astype(F32) * (
        _head_norm(haf_ref[...].astype(F32) + hab_ref[...].astype(F32), H_A, True) * na_ref[...])
    y_a = _dot(h_a.astype(BF16), woa_ref[...])
    h_b = (_head_norm(hbf_ref[...].astype(F32) + hbb_ref[...].astype(F32), H_B, False) * nb_ref[...]
           ) * sgb_ref[...].astype(F32)
    y_b = _dot(h_b.astype(BF16), wob_ref[...])
    merged = ga_ref[...].astype(F32) * y_a + gb_ref[...].astype(F32) * y_b
    mix = _dot(merged.astype(BF16), wo_ref[...])
    y = _layer_norm_rows(alpha * x_ref[...] + mix, g_ref[...], b_ref[...])
    o_ref[...] = y
    ob_ref[...] = y.astype(BF16)


def _mix_call(x, haf, hab, hbf, hbb, sg, sl, norm_a, norm_b, woa, wob, wo, g, b, alpha):
    t, d = x.shape
    tm = min(512, t)
    tile = lambda j=0: pl.BlockSpec((tm, d), lambda i: (i, j))
    vec = lambda: pl.BlockSpec((1, d), lambda i: (0, 0))
    mat = lambda: pl.BlockSpec((d, d), lambda i: (0, 0))
    r = lambda a: a.reshape(1, d)
    return pl.pallas_call(
        functools.partial(_mix_kernel, alpha=alpha),
        grid=(t // tm,),
        in_specs=[tile(), tile(), tile(), tile(), tile(),
                  tile(0), tile(1), tile(2), tile(1),
                  vec(), vec(), mat(), mat(), mat(), vec(), vec()],
        out_specs=[tile(), tile()],
        out_shape=[jax.ShapeDtypeStruct((t, d), F32), jax.ShapeDtypeStruct((t, d), BF16)],
        compiler_params=_cparams("parallel"),
        name="mix_out_ln1",
    )(x, haf, hab, hbf, hbb, sg, sg, sg, sl, r(norm_a), r(norm_b), woa, wob, wo, r(g), r(b))


def _shared_ple(xb, p, ws13_ref, ws2_ref, wple_ref, wpg_ref):
    h13 = _dot(xb, ws13_ref[...])
    hid = _silu(h13[:, :D_EXPERT]) * h13[:, D_EXPERT:]
    shared = _dot(hid.astype(BF16), ws2_ref[...])
    ple = _dot(p.astype(BF16), wple_ref[...]) * jax.nn.sigmoid(_dot(xb, wpg_ref[...]))
    return shared + ple


def _first_index(hit, iota, big):
    return jnp.min(jnp.where(hit, iota, big), axis=0, keepdims=True)


def _router_kernel(x_ref, wh_ref, wl_ref, bias_ref, idx_ref, wts_ref, pos_ref, cnt_ref, run_ref):
    @pl.when(pl.program_id(0) == 0)
    def _():
        run_ref[...] = jnp.zeros_like(run_ref)

    x = x_ref[...]
    xh = x.astype(BF16)
    xl = (x - xh.astype(F32)).astype(BF16)
    wh = wh_ref[...]
    logits = _dot_nt(wh, xh) + _dot_nt(wh, xl) + _dot_nt(wl_ref[...], xh)
    scores = jax.nn.sigmoid(logits)
    sel = scores + bias_ref[...]
    tm = sel.shape[1]
    ninf = -jnp.inf

    gi = lax.broadcasted_iota(jnp.int32, (GROUP_SIZE, tm), 0).astype(F32)
    g_rows = []
    for g in range(N_GROUPS):
        seg = sel[g * GROUP_SIZE:(g + 1) * GROUP_SIZE, :]
        m1 = jnp.max(seg, axis=0, keepdims=True)
        first = _first_index(seg == m1, gi, float(GROUP_SIZE))
        m2 = jnp.max(jnp.where(gi == first, ninf, seg), axis=0, keepdims=True)
        g_rows.append(m1 + m2)
    g_score = jnp.concatenate(g_rows, axis=0)

    ggi = lax.broadcasted_iota(jnp.int32, (N_GROUPS, tm), 0).astype(F32)
    g_keep = jnp.zeros((N_GROUPS, tm), F32)
    cur = g_score
    for _ in range(TOPK_GROUPS):
        mx = jnp.max(cur, axis=0, keepdims=True)
        hit = ggi == _first_index(cur == mx, ggi, float(N_GROUPS))
        g_keep = jnp.where(hit, 1.0, g_keep)
        cur = jnp.where(hit, ninf, cur)

    keep = jnp.concatenate(
        [jnp.broadcast_to(g_keep[g:g + 1, :], (GROUP_SIZE, tm)) for g in range(N_GROUPS)], axis=0)
    cur = jnp.where(keep > 0.0, sel, NEG)
    ei = lax.broadcasted_iota(jnp.int32, (N_EXPERTS, tm), 0).astype(F32)
    idx_rows, w_rows = [], []
    chosen = jnp.zeros((N_EXPERTS, tm), F32)
    for _ in range(TOP_K):
        mx = jnp.max(cur, axis=0, keepdims=True)
        first = _first_index(cur == mx, ei, float(N_EXPERTS))
        hit = ei == first
        idx_rows.append(first)
        w_rows.append(jnp.sum(jnp.where(hit, scores, 0.0), axis=0, keepdims=True))
        cur = jnp.where(hit, ninf, cur)
        chosen = jnp.where(hit, 1.0, chosen)
    w = jnp.concatenate(w_rows, axis=0)
    idx_ref[...] = jnp.concatenate(idx_rows, axis=0).astype(jnp.int32)
    wts_ref[...] = w / jnp.sum(w, axis=0, keepdims=True) * ROUTED_SCALE

    t_src = lax.broadcasted_iota(jnp.int32, (tm, tm), 0)
    t_dst = lax.broadcasted_iota(jnp.int32, (tm, tm), 1)
    before = (t_src < t_dst).astype(BF16)
    rank = _dot(chosen.astype(BF16), before) + run_ref[...]
    pos_ref[...] = jnp.concatenate(
        [jnp.sum(jnp.where(ei == r, rank, 0.0), axis=0, keepdims=True) for r in idx_rows],
        axis=0).astype(jnp.int32)
    run = run_ref[...] + jnp.sum(chosen, axis=1, keepdims=True)
    run_ref[...] = run
    cnt_ref[...] = run.astype(jnp.int32)


def _router_call(x, w_router, router_bias):
    t, d = x.shape
    tm = min(512, t)
    wt = w_router.T
    wh = wt.astype(BF16)
    wl = (wt - wh.astype(F32)).astype(BF16)
    kt = lambda: pl.BlockSpec((TOP_K, tm), lambda i: (0, i))
    return pl.pallas_call(
        _router_kernel,
        grid=(t // tm,),
        in_specs=[pl.BlockSpec((tm, d), lambda i: (i, 0)),
                  pl.BlockSpec((N_EXPERTS, d), lambda i: (0, 0)),
                  pl.BlockSpec((N_EXPERTS, d), lambda i: (0, 0)),
                  pl.BlockSpec((N_EXPERTS, 1), lambda i: (0, 0))],
        out_specs=[kt(), kt(), kt(), pl.BlockSpec((N_EXPERTS, 1), lambda i: (0, 0))],
        out_shape=[jax.ShapeDtypeStruct((TOP_K, t), jnp.int32),
                   jax.ShapeDtypeStruct((TOP_K, t), F32),
                   jax.ShapeDtypeStruct((TOP_K, t), jnp.int32),
                   jax.ShapeDtypeStruct((N_EXPERTS, 1), jnp.int32)],
        scratch_shapes=[pltpu.VMEM((N_EXPERTS, 1), F32)],
        compiler_params=_cparams("arbitrary"),
        name="router",
    )(x, wh, wl, router_bias.astype(F32).reshape(N_EXPERTS, 1))


def _dest_kernel(idx_ref, pos_ref, start_ref, dest_ref):
    tm = idx_ref.shape[1]
    ei = lax.broadcasted_iota(jnp.int32, (N_EXPERTS, tm), 0)
    start = start_ref[...].astype(F32)
    rows = [jnp.sum(jnp.where(ei == idx_ref[k:k + 1, :], start, 0.0), axis=0, keepdims=True)
            for k in range(TOP_K)]
    dest_ref[...] = pos_ref[...] + jnp.concatenate(rows, axis=0).astype(jnp.int32)


def _dest_call(idx_kt, pos_kt, p_start):
    t = idx_kt.shape[1]
    tm = min(1024, t)
    kt = lambda: pl.BlockSpec((TOP_K, tm), lambda i: (0, i))
    return pl.pallas_call(
        _dest_kernel,
        grid=(t // tm,),
        in_specs=[kt(), kt(), pl.BlockSpec((N_EXPERTS, 1), lambda i: (0, 0))],
        out_specs=kt(),
        out_shape=jax.ShapeDtypeStruct((TOP_K, t), jnp.int32),
        compiler_params=_cparams("parallel"),
        name="moe_dest",
    )(idx_kt, pos_kt, p_start.reshape(N_EXPERTS, 1))


LANES = 128
ROW_SUB = D_MODEL // (2 * LANES)
U32 = jnp.uint32
HIGH_HALF = 0xFFFF0000
DISPATCH_TILE = 256


def _row_tile(ref, r):
    return ref.at[pl.ds(pl.multiple_of(r * ROW_SUB, ROW_SUB), ROW_SUB), :]


def _lane_chunk(first_row, n_rows, j):
    return pl.ds(first_row * ROW_SUB + j, n_rows, stride=ROW_SUB)


def _bf16_bits(v):
    return lax.bitcast_convert_type(v.astype(BF16).astype(F32), U32)


def _to_row_tiles(dst_ref, v):
    for j in range(ROW_SUB):
        lo = _bf16_bits(v[:, (2 * j) * LANES:(2 * j + 1) * LANES]) >> 16
        hi = _bf16_bits(v[:, (2 * j + 1) * LANES:(2 * j + 2) * LANES])
        dst_ref[_lane_chunk(0, v.shape[0], j), :] = hi | lo


def _from_row_tiles(src_ref, first_row, n_rows):
    chunks = []
    for j in range(ROW_SUB):
        w = src_ref[_lane_chunk(first_row, n_rows, j), :]
        chunks.append(lax.bitcast_convert_type(w << 16, F32))
        chunks.append(lax.bitcast_convert_type(w & U32(HIGH_HALF), F32))
    return jnp.concatenate(chunks, axis=1)


def _dispatch_kernel(valid_ref, dest_ref, x_ref, p_ref, ws13_ref, ws2_ref, wple_ref, wpg_ref,
                     xs_hbm, dense_ref, xt, zbuf, sem, zsem):
    i = pl.program_id(0)
    tm = xt.shape[1] // ROW_SUB
    bm = zbuf.shape[0] // ROW_SUB
    n_blk = valid_ref.shape[0]
    slot = i % 2

    @pl.when(pl.program_id(0) == 0)
    def _():
        zbuf[...] = jnp.zeros_like(zbuf)

        def zero_copy(i):
            first = pl.multiple_of(i * (bm * ROW_SUB), bm * ROW_SUB)
            return pltpu.make_async_copy(zbuf, xs_hbm.at[pl.ds(first, bm * ROW_SUB), :], zsem)

        def start(i, carry):
            @pl.when(valid_ref[i] < bm)
            def _():
                zero_copy(i).start()
            return carry

        def wait(i, carry):
            @pl.when(valid_ref[i] < bm)
            def _():
                zero_copy(i).wait()
            return carry

        lax.fori_loop(0, n_blk, start, 0)
        lax.fori_loop(0, n_blk, wait, 0)

    stage = xt.at[slot]
    _to_row_tiles(stage, x_ref[...].astype(F32))

    def body(r, carry):
        for k in range(TOP_K):
            pltpu.make_async_copy(_row_tile(stage, r), _row_tile(xs_hbm, dest_ref[k, r]), sem.at[slot]).start()
        return carry

    lax.fori_loop(0, tm, body, 0)
    dense_ref[...] = _shared_ple(x_ref[...], p_ref[...], ws13_ref, ws2_ref, wple_ref, wpg_ref)

    def drain(s):
        for k in range(TOP_K):
            pltpu.make_async_copy(xt.at[s], xs_hbm.at[pl.ds(0, tm * ROW_SUB), :], sem.at[s]).wait()

    @pl.when(i > 0)
    def _():
        drain(1 - slot)

    @pl.when(i == pl.num_programs(0) - 1)
    def _():
        drain(slot)


def _dispatch_call(x, dest_kt, blk_valid, p, ws13, ws2, wple, wpg):
    t, d = x.shape
    tm = min(DISPATCH_TILE, t)
    bm = MOE_BLOCK
    n_blk = blk_valid.shape[0]
    full = lambda a: pl.BlockSpec(a.shape, lambda i, v: (0, 0))
    grid_spec = pltpu.PrefetchScalarGridSpec(
        num_scalar_prefetch=1,
        grid=(t // tm,),
        in_specs=[pl.BlockSpec((TOP_K, tm), lambda i, v: (0, i), memory_space=pltpu.SMEM),
                  pl.BlockSpec((tm, d), lambda i, v: (i, 0)),
                  pl.BlockSpec((tm, PLE_DIM), lambda i, v: (i, 0)),
                  full(ws13), full(ws2), full(wple), full(wpg)],
        out_specs=[pl.BlockSpec(memory_space=pl.ANY), pl.BlockSpec((tm, d), lambda i, v: (i, 0))],
        scratch_shapes=[pltpu.VMEM((2, tm * ROW_SUB, LANES), U32), pltpu.VMEM((bm * ROW_SUB, LANES), U32),
                        pltpu.SemaphoreType.DMA((2,)), pltpu.SemaphoreType.DMA(())],
    )
    return pl.pallas_call(
        _dispatch_kernel,
        grid_spec=grid_spec,
        out_shape=[jax.ShapeDtypeStruct((n_blk * bm * ROW_SUB, LANES), U32),
                   jax.ShapeDtypeStruct((t, d), F32)],
        compiler_params=_cparams("arbitrary"),
        name="moe_dispatch",
    )(blk_valid, dest_kt, x, p, ws13, ws2, wple, wpg)


def _expert_kernel(blk_e_ref, nused_ref, xs_ref, w13_ref, w2_ref, o_ref):
    i = pl.program_id(0)

    @pl.when(i < nused_ref[0])
    def _():
        xb = _from_row_tiles(xs_ref, 0, xs_ref.shape[0] // ROW_SUB).astype(BF16)
        h13 = _dot(xb, w13_ref[0])
        hid = _silu(h13[:, :D_EXPERT]) * h13[:, D_EXPERT:]
        _to_row_tiles(o_ref, _dot(hid.astype(BF16), w2_ref[0]))

    @pl.when(i >= nused_ref[0])
    def _():
        o_ref[...] = jnp.zeros_like(o_ref)


def _expert_call(xs, blk_e, n_used, w13, w2):
    bm = MOE_BLOCK
    n_blk = blk_e.shape[0]
    d = D_MODEL
    xs_idx = lambda i, e, n: (jnp.minimum(i, jnp.maximum(n[0] - 1, 0)), 0)
    grid_spec = pltpu.PrefetchScalarGridSpec(
        num_scalar_prefetch=2,
        grid=(n_blk,),
        in_specs=[pl.BlockSpec((bm * ROW_SUB, LANES), xs_idx),
                  pl.BlockSpec((1, d, 2 * D_EXPERT), lambda i, e, n: (e[i], 0, 0)),
                  pl.BlockSpec((1, D_EXPERT, d), lambda i, e, n: (e[i], 0, 0))],
        out_specs=pl.BlockSpec((bm * ROW_SUB, LANES), lambda i, e, n: (i, 0)),
    )
    return pl.pallas_call(
        _expert_kernel,
        grid_spec=grid_spec,
        out_shape=jax.ShapeDtypeStruct((n_blk * bm * ROW_SUB, LANES), U32),
        compiler_params=_cparams("arbitrary"),
        name="routed_experts",
    )(blk_e, n_used, xs, w13, w2)


def _combine_kernel(dest_ref, destn_ref, ys_hbm, w_ref, dense_ref, x_ref, g_ref, b_ref,
                    o_ref, ob_ref, gbuf, sem, *, alpha):
    i = pl.program_id(0)
    n = pl.num_programs(0)
    rows = gbuf.shape[1] // ROW_SUB
    tm = rows // TOP_K
    cur = i % 2

    def issue(d_ref, s):
        def body(r, carry):
            for k in range(TOP_K):
                pltpu.make_async_copy(_row_tile(ys_hbm, d_ref[k, r]), _row_tile(gbuf.at[s], k * tm + r),
                                      sem.at[s]).start()
            return carry
        lax.fori_loop(0, tm, body, 0)

    @pl.when(i == 0)
    def _():
        issue(dest_ref, 0)

    @pl.when(i + 1 < n)
    def _():
        issue(destn_ref, 1 - cur)

    pltpu.make_async_copy(ys_hbm.at[pl.ds(0, rows * ROW_SUB), :], gbuf.at[cur], sem.at[cur]).wait()
    w = w_ref[...]
    routed = _from_row_tiles(gbuf.at[cur], 0, tm) * w[:, 0:1]
    for k in range(1, TOP_K):
        routed = routed + _from_row_tiles(gbuf.at[cur], k * tm, tm) * w[:, k:k + 1]
    y = _layer_norm_rows(alpha * x_ref[...] + routed + dense_ref[...], g_ref[...], b_ref[...])
    o_ref[...] = y
    ob_ref[...] = y.astype(BF16)


def _combine_call(ys, dest_kt, wts, dense, x, g, b, alpha):
    t, d = x.shape
    tm = min(COMBINE_TILE, t)
    nt = t // tm
    rows = tm * TOP_K
    tile = lambda: pl.BlockSpec((tm, d), lambda i: (i, 0))
    vec = lambda: pl.BlockSpec((1, d), lambda i: (0, 0))
    return pl.pallas_call(
        functools.partial(_combine_kernel, alpha=alpha),
        grid=(nt,),
        in_specs=[pl.BlockSpec((TOP_K, tm), lambda i: (0, i), memory_space=pltpu.SMEM),
                  pl.BlockSpec((TOP_K, tm), lambda i: (0, jnp.minimum(i + 1, nt - 1)),
                               memory_space=pltpu.SMEM),
                  pl.BlockSpec(memory_space=pl.ANY),
                  pl.BlockSpec((tm, TOP_K), lambda i: (i, 0)),
                  tile(), tile(), vec(), vec()],
        out_specs=[tile(), tile()],
        out_shape=[jax.ShapeDtypeStruct((t, d), F32), jax.ShapeDtypeStruct((t, d), BF16)],
        scratch_shapes=[pltpu.VMEM((2, rows * ROW_SUB, LANES), U32), pltpu.SemaphoreType.DMA((2,))],
        compiler_params=_cparams("arbitrary"),
        name="combine_ln2",
    )(dest_kt, dest_kt, ys, wts, dense, x, g.reshape(1, d), b.reshape(1, d))


def _block_tables(counts, n_blk):
    bm = MOE_BLOCK
    padded = (counts + bm - 1) // bm * bm
    p_end = jnp.cumsum(padded)
    p_start = p_end - padded
    blk_row = jnp.arange(n_blk, dtype=jnp.int32) * bm
    blk_e = jnp.minimum(jnp.sum(p_end[None, :] <= blk_row[:, None], axis=1), N_EXPERTS - 1).astype(jnp.int32)
    blk_valid = jnp.clip(counts[blk_e] - (blk_row - p_start[blk_e]), 0, bm).astype(jnp.int32)
    n_used = (p_end[-1] // bm).astype(jnp.int32).reshape(1)
    return p_start.astype(jnp.int32), blk_e, blk_valid, n_used


def _prep_layer(li, lower, w_in, b_in, conv_w, conv_b, norm_a, norm_b, w_out_a, w_out_b, w_o,
                ln1_g, ln1_b, w_ple, w_ple_gate, w_router, router_bias, w1, w3, w2, ws1, ws3, ws2,
                ln2_g, ln2_b):
    d = D_MODEL
    w = w_in[li]
    bi = b_in[li]
    seg = lambda a, lo, hi: a[..., lo:hi]
    o_qkv, o_oa, o_g = 0, 3 * d, 4 * d
    o_qb = 4 * d + 4 * H_A
    o_ff, o_ib, o_gb, o_ga = o_qb + d, o_qb + 3 * d, o_qb + 4 * d, o_qb + 5 * d
    cat = lambda parts: jnp.concatenate(parts, axis=-1)
    raw_cols = [(o_qkv, o_qkv + 3 * d), (o_ib, o_ib + d)]
    sig_cols = [(o_oa, o_oa + d), (o_ga, o_ga + 2 * d)]
    silu_cols = [(o_qb, o_qb + d), (o_gb, o_gb + d)]
    return dict(
        w_raw=cat([seg(w, *c) for c in raw_cols]).astype(BF16), b_raw=cat([seg(bi, *c) for c in raw_cols]),
        w_sig=cat([seg(w, *c) for c in sig_cols]).astype(BF16), b_sig=cat([seg(bi, *c) for c in sig_cols]),
        w_silu=cat([seg(w, *c) for c in silu_cols]).astype(BF16), b_silu=cat([seg(bi, *c) for c in silu_cols]),
        w_fg=seg(w, o_ff, o_ff + 2 * d).astype(BF16), b_fg=seg(bi, o_ff, o_ff + 2 * d),
        w_g=seg(w, o_g, o_g + 4 * H_A).astype(BF16), b_g=seg(bi, o_g, o_g + 4 * H_A),
        lower=lower[li].reshape(-1),
        conv_w=conv_w[li], conv_b=conv_b[li], norm_a=norm_a[li], norm_b=norm_b[li],
        w_out_a=w_out_a[li].astype(BF16), w_out_b=w_out_b[li].astype(BF16), w_o=w_o[li].astype(BF16),
        ln1_g=ln1_g[li], ln1_b=ln1_b[li],
        w_ple=w_ple[li].astype(BF16), w_ple_gate=w_ple_gate[li].astype(BF16),
        w_router=w_router[li], router_bias=router_bias[li],
        w13=jnp.concatenate([w1[li], w3[li]], axis=-1).astype(BF16), w2=w2[li].astype(BF16),
        ws13=jnp.concatenate([ws1[li], ws3[li]], axis=-1).astype(BF16), ws2=ws2[li].astype(BF16),
        ln2_g=ln2_g[li], ln2_b=ln2_b[li],
    )


def _layer(x, xb, p, lp, bsz, s, alpha):
    t, d = x.shape
    a = _proj_act(xb, lp["w_raw"], lp["b_raw"], "raw", "proj_raw")
    sg = _proj_act(xb, lp["w_sig"], lp["b_sig"], "sigmoid", "proj_sigmoid")
    sl = _proj_act(xb, lp["w_silu"], lp["b_silu"], "silu", "proj_silu")
    kf, lf = _proj_forget(xb, lp["w_fg"], lp["b_fg"], lp["lower"])
    gc, gr = _proj_gates(xb, lp["w_g"], lp["b_g"])

    a3 = a.reshape(bsz, s, 4 * d)
    qk3 = _conv_silu(a3, lp["conv_w"], lp["conv_b"])
    gc3 = gc.reshape(bsz, s, 4 * H_A)
    haf = _mlstm_dir(qk3, a3, gc3, gr, False).reshape(t, d)
    hab = _mlstm_dir(qk3, a3, gc3, gr, True).reshape(t, d)

    sl3 = sl.reshape(bsz, s, 2 * d)
    kf3 = kf.reshape(bsz, s, 2 * d)
    lf3 = lf.reshape(bsz, s, 2 * d)
    hbf = _hgrn_dir(sl3, kf3, a3, lf3, False).reshape(t, d)
    hbb = _hgrn_dir(sl3, kf3, a3, lf3, True).reshape(t, d)

    x1, x1b = _mix_call(x, haf, hab, hbf, hbb, sg, sl, lp["norm_a"], lp["norm_b"],
                        lp["w_out_a"], lp["w_out_b"], lp["w_o"], lp["ln1_g"], lp["ln1_b"], alpha)

    idx_kt, wts_kt, pos_kt, counts = _router_call(x1, lp["w_router"], lp["router_bias"])
    n_blk = t * TOP_K // MOE_BLOCK + N_EXPERTS
    p_start, blk_e, blk_valid, n_used = _block_tables(counts.reshape(-1), n_blk)
    dest_kt = _dest_call(idx_kt, pos_kt, p_start)
    xs, dense = _dispatch_call(x1b, dest_kt, blk_valid, p, lp["ws13"], lp["ws2"], lp["w_ple"], lp["w_ple_gate"])
    ys = _expert_call(xs, blk_e, n_used, lp["w13"], lp["w2"])
    return _combine_call(ys, dest_kt, wts_kt.T, dense, x1, lp["ln2_g"], lp["ln2_b"], alpha)


def kernel(x_prompt, x_sample, p_prompt, p_sample, emb_ln_g, emb_ln_b, w_in, b_in, conv_w, conv_b, norm_a, norm_b, lb_param, w_out_a, w_out_b, w_o, ln1_g, ln1_b, w_ple, w_ple_gate, w_router, router_bias, w1, w3, w2, ws1, ws3, ws2, ln2_g, ln2_b):
    depth = w_in.shape[0]
    alpha = (2.0 * depth) ** 0.25
    lb_soft = jax.nn.softmax(lb_param.astype(F32), axis=0)
    lower = jnp.cumsum(lb_soft, axis=0) - lb_soft[0:1]
    layers = [_prep_layer(li, lower, w_in, b_in, conv_w, conv_b, norm_a, norm_b, w_out_a, w_out_b, w_o,
                          ln1_g, ln1_b, w_ple, w_ple_gate, w_router, router_bias, w1, w3, w2,
                          ws1, ws3, ws2, ln2_g, ln2_b) for li in range(depth)]

    def trunk(x3, p4):
        bsz, s, d = x3.shape
        x, xb = _layer_norm_call(x3.reshape(bsz * s, d), emb_ln_g, emb_ln_b)
        for li in range(depth):
            x, xb = _layer(x, xb, p4[li].reshape(bsz * s, -1), layers[li], bsz, s, alpha)
        return x.reshape(bsz, s, d)

    return (trunk(x_prompt, p_prompt), trunk(x_sample, p_sample))
```

```python
import functools

import jax
import jax.numpy as jnp
from jax import lax
from jax.experimental import pallas as pl
from jax.experimental.pallas import tpu as pltpu

F32 = jnp.float32
BF16 = jnp.bfloat16

D_MODEL = 1024
PLE_DIM = 256
H_A = 4
DH_A = D_MODEL // H_A
CHUNK_A = 128
H_B = 8
DK_B = 128
CHUNK_B = 128
N_EXPERTS = 256
TOP_K = 8
N_GROUPS = 8
GROUP_SIZE = N_EXPERTS // N_GROUPS
TOPK_GROUPS = 4
D_EXPERT = 256
ROUTED_SCALE = 2.5
EPS = 1e-5
NEG = -1e30
TINY = 1e-30

HGRN_SAFE_RANGE = 80.0
HGRN_ROWS = CHUNK_B
MOE_BLOCK = 512
COMBINE_TILE = 128
VMEM_LIMIT = 56 * 1024 * 1024


def _cparams(*sem):
    return pltpu.CompilerParams(dimension_semantics=sem, vmem_limit_bytes=VMEM_LIMIT)


def _split3(x):
    hi = x.astype(BF16)
    r1 = x - hi.astype(F32)
    mid = r1.astype(BF16)
    lo = (r1 - mid.astype(F32)).astype(BF16)
    return hi, mid, lo


def _dot(a, b):
    return jnp.dot(a, b, preferred_element_type=F32)


def _dot_nt(a, b):
    return lax.dot_general(a, b, (((1,), (1,)), ((), ())), preferred_element_type=F32)


def _dot_tn(a, b):
    return lax.dot_general(a, b, (((0,), (0,)), ((), ())), preferred_element_type=F32)


def _layer_norm_rows(v, g, b):
    mu = jnp.mean(v, axis=-1, keepdims=True)
    vc = v - mu
    var = jnp.mean(vc * vc, axis=-1, keepdims=True)
    return vc * lax.rsqrt(var + EPS) * g + b


def _log_sigmoid(z):
    return jnp.minimum(z, 0.0) - jnp.log1p(jnp.exp(-jnp.abs(z)))


def _silu(z):
    return z * jax.nn.sigmoid(z)


def _ln_kernel(x_ref, g_ref, b_ref, o_ref, ob_ref):
    y = _layer_norm_rows(x_ref[...], g_ref[...], b_ref[...])
    o_ref[...] = y
    ob_ref[...] = y.astype(BF16)


def _layer_norm_call(x, g, b):
    t, d = x.shape
    tm = min(512, t)
    return pl.pallas_call(
        _ln_kernel,
        grid=(t // tm,),
        in_specs=[pl.BlockSpec((tm, d), lambda i: (i, 0)),
                  pl.BlockSpec((1, d), lambda i: (0, 0)),
                  pl.BlockSpec((1, d), lambda i: (0, 0))],
        out_specs=[pl.BlockSpec((tm, d), lambda i: (i, 0)),
                   pl.BlockSpec((tm, d), lambda i: (i, 0))],
        out_shape=[jax.ShapeDtypeStruct((t, d), F32), jax.ShapeDtypeStruct((t, d), BF16)],
        compiler_params=_cparams("parallel"),
        name="emb_layer_norm",
    )(x, g.reshape(1, d), b.reshape(1, d))


PROJ_SUB = 256


def _row_blocks(n):
    step = min(PROJ_SUB, n)
    return [slice(r, r + step) for r in range(0, n, step)]


def _proj_act_kernel(x_ref, w_ref, b_ref, o_ref, *, act):
    for rs in _row_blocks(x_ref.shape[0]):
        z = _dot(x_ref[rs, :], w_ref[...]) + b_ref[...]
        if act == "sigmoid":
            z = jax.nn.sigmoid(z)
        elif act == "silu":
            z = _silu(z)
        o_ref[rs, :] = z.astype(o_ref.dtype)


def _proj_act(xb, w, b, act, name):
    t, d = xb.shape
    n = w.shape[1]
    tm = min(1024, t)
    tn = 1024
    return pl.pallas_call(
        functools.partial(_proj_act_kernel, act=act),
        grid=(t // tm, n // tn),
        in_specs=[pl.BlockSpec((tm, d), lambda i, j: (i, 0)),
                  pl.BlockSpec((d, tn), lambda i, j: (0, j)),
                  pl.BlockSpec((1, tn), lambda i, j: (0, j))],
        out_specs=pl.BlockSpec((tm, tn), lambda i, j: (i, j)),
        out_shape=jax.ShapeDtypeStruct((t, n), BF16),
        compiler_params=_cparams("parallel", "arbitrary"),
        name=name,
    )(xb, w, b.reshape(1, n))


def _proj_forget_kernel(x_ref, w_ref, b_ref, oml_ref, la_ref, lc_ref, k_ref, lf_ref):
    a = la_ref[...]
    for rs in _row_blocks(x_ref.shape[0]):
        z = _dot(x_ref[rs, :], w_ref[...]) + b_ref[...]
        e = jnp.exp(-jnp.abs(z))
        r = 1.0 / (1.0 + e)
        k_ref[rs, :] = (oml_ref[...] * (jnp.where(z > 0.0, e, 1.0) * r)).astype(BF16)
        u = lc_ref[...] + (jnp.minimum(z, 0.0) + jnp.log(r))
        lf_ref[rs, :] = jnp.maximum(a, u) + jnp.log(1.0 + jnp.exp(-jnp.abs(a - u)))


def _proj_forget(xb, w, b, lower):
    t, d = xb.shape
    n = w.shape[1]
    tm = min(1024, t)
    tn = 1024
    lb = lower.reshape(1, n)
    vec = lambda: pl.BlockSpec((1, tn), lambda i, j: (0, j))
    return pl.pallas_call(
        _proj_forget_kernel,
        grid=(t // tm, n // tn),
        in_specs=[pl.BlockSpec((tm, d), lambda i, j: (i, 0)),
                  pl.BlockSpec((d, tn), lambda i, j: (0, j)),
                  vec(), vec(), vec(), vec()],
        out_specs=[pl.BlockSpec((tm, tn), lambda i, j: (i, j)),
                   pl.BlockSpec((tm, tn), lambda i, j: (i, j))],
        out_shape=[jax.ShapeDtypeStruct((t, n), BF16), jax.ShapeDtypeStruct((t, n), F32)],
        compiler_params=_cparams("parallel", "arbitrary"),
        name="proj_forget",
    )(xb, w, b.reshape(1, n), 1.0 - lb, jnp.log(jnp.maximum(lb, TINY)), jnp.log1p(-lb))


def _proj_gates_kernel(x_ref, w_ref, wt_ref, b_ref, bt_ref, gc_ref, gr_ref):
    x = x_ref[...]
    zc = _dot(x, w_ref[...]) + b_ref[...]
    lane = lax.broadcasted_iota(jnp.int32, zc.shape, 1)
    gc_ref[...] = jnp.where(lane >= 2 * H_A, _log_sigmoid(zc), zc)
    zr = _dot_nt(wt_ref[...], x) + bt_ref[...]
    sub = lax.broadcasted_iota(jnp.int32, zr.shape, 0)
    gr_ref[...] = jnp.where(sub >= 2 * H_A, _log_sigmoid(zr), zr)


def _proj_gates(xb, w, b):
    t, d = xb.shape
    n = w.shape[1]
    tm = min(1024, t)
    return pl.pallas_call(
        _proj_gates_kernel,
        grid=(t // tm,),
        in_specs=[pl.BlockSpec((tm, d), lambda i: (i, 0)),
                  pl.BlockSpec((d, n), lambda i: (0, 0)),
                  pl.BlockSpec((n, d), lambda i: (0, 0)),
                  pl.BlockSpec((1, n), lambda i: (0, 0)),
                  pl.BlockSpec((n, 1), lambda i: (0, 0))],
        out_specs=[pl.BlockSpec((tm, n), lambda i: (i, 0)),
                   pl.BlockSpec((n, tm), lambda i: (0, i))],
        out_shape=[jax.ShapeDtypeStruct((t, n), F32), jax.ShapeDtypeStruct((n, t), F32)],
        compiler_params=_cparams("parallel"),
        name="proj_gates",
    )(xb, w, w.T, b.reshape(1, n), b.reshape(n, 1))


def _conv_kernel(x_ref, w_ref, b_ref, o_ref, *, rows):
    s = x_ref.shape[1]
    n_rc = s // rows
    w = w_ref[...]
    w0, w1, w2 = w[0:1, :], w[1:2, :], w[2:3, :]
    bias = b_ref[...]
    ridx = lax.broadcasted_iota(jnp.int32, (rows, x_ref.shape[2]), 0)

    def body(r, carry):
        base = pl.multiple_of(r * rows, rows)
        cur = x_ref[0, pl.ds(base, rows), :].astype(F32)
        pstart = pl.multiple_of(jnp.maximum(base - 16, 0), 16)
        nstart = pl.multiple_of(jnp.minimum(base + rows, s - 16), 16)
        prev_row = x_ref[0, pl.ds(pstart, 16), :][15:16, :].astype(F32)
        next_row = x_ref[0, pl.ds(nstart, 16), :][0:1, :].astype(F32)
        prev_row = jnp.where(r > 0, prev_row, 0.0)
        next_row = jnp.where(r < n_rc - 1, next_row, 0.0)
        up = jnp.where(ridx == 0, prev_row, pltpu.roll(cur, 1, 0))
        dn = jnp.where(ridx == rows - 1, next_row, pltpu.roll(cur, rows - 1, 0))
        z = w0 * up + w1 * cur + w2 * dn + bias
        o_ref[0, pl.ds(base, rows), :] = _silu(z).astype(o_ref.dtype)
        return carry

    lax.fori_loop(0, n_rc, body, 0)


def _conv_silu(a3, conv_w, conv_b):
    bsz, s, _ = a3.shape
    c = conv_w.shape[1]
    cw = 256
    rows = min(256, s)
    return pl.pallas_call(
        functools.partial(_conv_kernel, rows=rows),
        grid=(bsz, c // cw),
        in_specs=[pl.BlockSpec((1, s, cw), lambda b, j: (b, 0, j)),
                  pl.BlockSpec((3, cw), lambda b, j: (0, j)),
                  pl.BlockSpec((1, cw), lambda b, j: (0, j))],
        out_specs=pl.BlockSpec((1, s, cw), lambda b, j: (b, 0, j)),
        out_shape=jax.ShapeDtypeStruct((bsz, s, c), BF16),
        compiler_params=_cparams("parallel", "parallel"),
        name="conv_silu",
    )(a3, conv_w, conv_b.reshape(1, c))


def _mlstm_kernel(q_ref, k_ref, v_ref, gc_ref, gr_ref, o_ref, c_ref, n_ref, m_ref, *, reverse):
    L = CHUNK_A

    @pl.when(pl.program_id(1) == 0)
    def _():
        c_ref[...] = jnp.zeros_like(c_ref)
        n_ref[...] = jnp.zeros_like(n_ref)
        m_ref[...] = jnp.zeros_like(m_ref)

    row = lax.broadcasted_iota(jnp.int32, (L, L), 0)
    col = lax.broadcasted_iota(jnp.int32, (L, L), 1)
    if reverse:
        mask = col >= row
        mask_t = row >= col
        last = 0
    else:
        mask = col <= row
        mask_t = row <= col
        last = L - 1
    m_mat = mask.astype(BF16)
    mt_mat = mask_t.astype(BF16)

    gc = gc_ref[0]
    gr = gr_ref[...]
    cum_c = sum(_dot(m_mat, p) for p in _split3(gc))
    cum_r = sum(_dot(p, mt_mat) for p in _split3(gr))
    off = H_A if reverse else 0
    scale = DH_A ** -0.5

    for h in range(H_A):
        ii = off + h
        fi = 2 * H_A + off + h
        hs = slice(h * DH_A, (h + 1) * DH_A)
        bc = cum_c[:, fi:fi + 1]
        br = cum_r[fi:fi + 1, :]
        ic = gc[:, ii:ii + 1]
        ir = gr[ii:ii + 1, :]
        m_st = m_ref[h:h + 1, 0:1]
        qh = q_ref[0, :, hs]
        kh = k_ref[0, :, hs]
        vh = v_ref[0, :, hs]
        c_st = c_ref[h]
        n_st = n_ref[h:h + 1, :]

        d_mat = jnp.where(mask, bc - br + ir, NEG)
        m_inter = bc + m_st
        m_row = jnp.maximum(m_inter, jnp.max(d_mat, axis=1, keepdims=True))
        w_inter = jnp.exp(m_inter - m_row)
        s_mat = _dot_nt(qh, kh) * scale * jnp.exp(d_mat - m_row)
        num = w_inter * _dot(qh, c_st.astype(BF16)) + _dot(s_mat.astype(BF16), vh)
        qn = jnp.sum(qh.astype(F32) * n_st, axis=1, keepdims=True)
        den = w_inter * qn + jnp.sum(s_mat, axis=1, keepdims=True)
        inv = 1.0 / jnp.maximum(jnp.abs(den), jnp.exp(-m_row))
        o_ref[0, :, hs] = (num * inv).astype(o_ref.dtype)

        b_last = bc[last:last + 1, :]
        g_end_r = b_last - br + ir
        g_end_c = b_last - bc + ic
        m_next = jnp.maximum(b_last + m_st, jnp.max(g_end_r, axis=1, keepdims=True))
        decay = jnp.exp(b_last + m_st - m_next)
        wk = jnp.exp(g_end_c - m_next) * (kh.astype(F32) * scale)
        c_ref[h] = decay * c_st + _dot_tn(wk.astype(BF16), vh)
        n_ref[h:h + 1, :] = decay * n_st + jnp.sum(wk, axis=0, keepdims=True)
        m_ref[h:h + 1, :] = jnp.broadcast_to(m_next, (1, m_ref.shape[1]))


def _mlstm_dir(qk3, a3, gc3, gr, reverse):
    bsz, s, _ = qk3.shape
    L = CHUNK_A
    nc = s // L
    cidx = (lambda c: nc - 1 - c) if reverse else (lambda c: c)
    return pl.pallas_call(
        functools.partial(_mlstm_kernel, reverse=reverse),
        grid=(bsz, nc),
        in_specs=[pl.BlockSpec((1, L, D_MODEL), lambda b, c: (b, cidx(c), 0)),
                  pl.BlockSpec((1, L, D_MODEL), lambda b, c: (b, cidx(c), 1)),
                  pl.BlockSpec((1, L, D_MODEL), lambda b, c: (b, cidx(c), 2)),
                  pl.BlockSpec((1, L, 4 * H_A), lambda b, c: (b, cidx(c), 0)),
                  pl.BlockSpec((4 * H_A, L), lambda b, c: (0, b * nc + cidx(c)))],
        out_specs=pl.BlockSpec((1, L, D_MODEL), lambda b, c: (b, cidx(c), 0)),
        out_shape=jax.ShapeDtypeStruct((bsz, s, D_MODEL), BF16),
        scratch_shapes=[pltpu.VMEM((H_A, DH_A, DH_A), F32),
                        pltpu.VMEM((H_A, DH_A), F32),
                        pltpu.VMEM((H_A, 128), F32)],
        compiler_params=_cparams("parallel", "arbitrary"),
        name="mlstm_bwd" if reverse else "mlstm_fwd",
    )(qk3, qk3, a3, gc3, gr)


def _hgrn_kernel(q_ref, k_ref, v_ref, lf_ref, o_ref, st_ref, oi_ref, b_ref, q32_ref, k32_ref, v32_ref, *,
                 reverse):
    L = CHUNK_B
    rows_blk = q_ref.shape[1]
    n_sub = rows_blk // L
    width = q_ref.shape[2]

    @pl.when(pl.program_id(1) == 0)
    def _():
        st_ref[...] = jnp.zeros_like(st_ref)

    row = lax.broadcasted_iota(jnp.int32, (rows_blk, rows_blk), 0)
    col = lax.broadcasted_iota(jnp.int32, (rows_blk, rows_blk), 1)
    vis = (col >= row) if reverse else (col <= row)
    same = (row // L) == (col // L)
    last = 0 if reverse else L - 1
    mask = vis[0:L, 0:L]

    m_all = jnp.logical_and(vis, same).astype(BF16)
    b = sum(_dot(m_all, p) for p in _split3(lf_ref[0]))

    def per_chunk_row(r):
        return jnp.concatenate(
            [jnp.broadcast_to(b[s * L + r:s * L + r + 1, :], (L, width)) for s in range(n_sub)], axis=0)

    q = q_ref[0].astype(F32)
    k = k_ref[0].astype(F32)
    v = v_ref[0]
    q_in = (q * jnp.exp(b)).astype(BF16)
    k_out = (k * jnp.exp(per_chunk_row(last) - b)).astype(BF16)
    dev = b - per_chunk_row(L // 2)
    rng = jnp.max(jnp.abs(dev))
    qt = (q * jnp.exp(dev)).astype(BF16)
    kt = (k * jnp.exp(-dev)).astype(BF16)

    for s in (reversed(range(n_sub)) if reverse else range(n_sub)):
        rs = slice(s * L, (s + 1) * L)
        dec = jnp.exp(b[s * L + last:s * L + last + 1, :])
        for h in range(H_B):
            hs = slice(h * DK_B, (h + 1) * DK_B)
            st = st_ref[h]
            o_inter = _dot_nt(q_in[rs, hs], st.astype(BF16))
            a = jnp.where(mask, _dot_nt(qt[rs, hs], kt[rs, hs]), 0.0)
            o_ref[0, rs, hs] = (o_inter + _dot(a.astype(BF16), v[rs, hs])).astype(o_ref.dtype)
            oi_ref[rs, hs] = o_inter
            st_ref[h] = dec[:, hs] * st + _dot_tn(v[rs, hs], k_out[rs, hs])

    @pl.when(rng > HGRN_SAFE_RANGE)
    def _():
        b_ref[...] = b
        q32_ref[...] = q
        k32_ref[...] = k
        v32_ref[...] = v.astype(F32)
        sidx = lax.broadcasted_iota(jnp.int32, (rows_blk, 1), 0)

        def body(j, carry):
            is_j = sidx == j
            b_all = b_ref[...]
            bj = jnp.sum(jnp.where(is_j, b_all, 0.0), axis=0, keepdims=True)
            qj = jnp.sum(jnp.where(is_j, q32_ref[...], 0.0), axis=0, keepdims=True)
            valid = jnp.logical_and((sidx >= j) if reverse else (sidx <= j), sidx // L == j // L)
            e = jnp.exp(jnp.minimum(bj - b_all, 0.0))
            t = jnp.where(valid, k32_ref[...] * e * qj, 0.0)
            for h in range(H_B):
                hs = slice(h * DK_B, (h + 1) * DK_B)
                a = jnp.sum(t[:, hs], axis=1, keepdims=True)
                row_j = jnp.sum(a * v32_ref[:, hs], axis=0, keepdims=True)
                oi_ref[:, hs] += jnp.where(is_j, row_j, 0.0)
            return carry

        lax.fori_loop(0, rows_blk, body, 0)
        o_ref[0] = oi_ref[...].astype(o_ref.dtype)


def _hgrn_dir(qs3, kf3, a3, lf3, reverse):
    bsz, s, _ = qs3.shape
    L = HGRN_ROWS
    nc = s // L
    w = D_MODEL
    cidx = (lambda c: nc - 1 - c) if reverse else (lambda c: c)
    d = 1 if reverse else 0
    return pl.pallas_call(
        functools.partial(_hgrn_kernel, reverse=reverse),
        grid=(bsz, nc),
        in_specs=[pl.BlockSpec((1, L, w), lambda b, c: (b, cidx(c), 0)),
                  pl.BlockSpec((1, L, w), lambda b, c: (b, cidx(c), d)),
                  pl.BlockSpec((1, L, w), lambda b, c: (b, cidx(c), 3)),
                  pl.BlockSpec((1, L, w), lambda b, c: (b, cidx(c), d))],
        out_specs=pl.BlockSpec((1, L, w), lambda b, c: (b, cidx(c), 0)),
        out_shape=jax.ShapeDtypeStruct((bsz, s, w), BF16),
        scratch_shapes=[pltpu.VMEM((H_B, DK_B, DK_B), F32)] + [pltpu.VMEM((L, w), F32)] * 5,
        compiler_params=_cparams("parallel", "arbitrary"),
        name="hgrn_bwd" if reverse else "hgrn_fwd",
    )(qs3, kf3, a3, lf3)


def _head_norm(v, n_heads, center):
    width = v.shape[1] // n_heads
    parts = []
    for h in range(n_heads):
        seg = v[:, h * width:(h + 1) * width]
        if center:
            seg = seg - jnp.mean(seg, axis=1, keepdims=True)
        parts.append(seg * lax.rsqrt(jnp.mean(seg * seg, axis=1, keepdims=True) + EPS))
    return jnp.concatenate(parts, axis=1)


def _mix_kernel(x_ref, haf_ref, hab_ref, hbf_ref, hbb_ref, og_ref, ga_ref, gb_ref, sgb_ref,
                na_ref, nb_ref, woa_ref, wob_ref, wo_ref, g_ref, b_ref, o_ref, ob_ref, *, alpha):
    h_a = og_ref[...].astype(F32) * (
        _head_norm(haf_ref[...].astype(F32) + hab_ref[...].astype(F32), H_A, True) * na_ref[...])
    y_a = _dot(h_a.astype(BF16), woa_ref[...])
    h_b = (_head_norm(hbf_ref[...].astype(F32) + hbb_ref[...].astype(F32), H_B, False) * nb_ref[...]
           ) * sgb_ref[...].astype(F32)
    y_b = _dot(h_b.astype(BF16), wob_ref[...])
    merged = ga_ref[...].astype(F32) * y_a + gb_ref[...].astype(F32) * y_b
    mix = _dot(merged.astype(BF16), wo_ref[...])
    y = _layer_norm_rows(alpha * x_ref[...] + mix, g_ref[...], b_ref[...])
    o_ref[...] = y
    ob_ref[...] = y.astype(BF16)


def _mix_call(x, haf, hab, hbf, hbb, sg, sl, norm_a, norm_b, woa, wob, wo, g, b, alpha):
    t, d = x.shape
    tm = min(512, t)
    tile = lambda j=0: pl.BlockSpec((tm, d), lambda i: (i, j))
    vec = lambda: pl.BlockSpec((1, d), lambda i: (0, 0))
    mat = lambda: pl.BlockSpec((d, d), lambda i: (0, 0))
    r = lambda a: a.reshape(1, d)
    return pl.pallas_call(
        functools.partial(_mix_kernel, alpha=alpha),
        grid=(t // tm,),
        in_specs=[tile(), tile(), tile(), tile(), tile(),
                  tile(0), tile(1), tile(2), tile(1),
                  vec(), vec(), mat(), mat(), mat(), vec(), vec()],
        out_specs=[tile(), tile()],
        out_shape=[jax.ShapeDtypeStruct((t, d), F32), jax.ShapeDtypeStruct((t, d), BF16)],
        compiler_params=_cparams("parallel"),
        name="mix_out_ln1",
    )(x, haf, hab, hbf, hbb, sg, sg, sg, sl, r(norm_a), r(norm_b), woa, wob, wo, r(g), r(b))


def _shared_ple(xb, p, ws13_ref, ws2_ref, wple_ref, wpg_ref):
    h13 = _dot(xb, ws13_ref[...])
    hid = _silu(h13[:, :D_EXPERT]) * h13[:, D_EXPERT:]
    shared = _dot(hid.astype(BF16), ws2_ref[...])
    ple = _dot(p.astype(BF16), wple_ref[...]) * jax.nn.sigmoid(_dot(xb, wpg_ref[...]))
    return shared + ple


def _first_index(hit, iota, big):
    return jnp.min(jnp.where(hit, iota, big), axis=0, keepdims=True)


def _router_kernel(x_ref, wh_ref, wl_ref, bias_ref, idx_ref, wts_ref, pos_ref, cnt_ref, run_ref):
    @pl.when(pl.program_id(0) == 0)
    def _():
        run_ref[...] = jnp.zeros_like(run_ref)

    x = x_ref[...]
    xh = x.astype(BF16)
    xl = (x - xh.astype(F32)).astype(BF16)
    wh = wh_ref[...]
    logits = _dot_nt(wh, xh) + _dot_nt(wh, xl) + _dot_nt(wl_ref[...], xh)
    scores = jax.nn.sigmoid(logits)
    sel = scores + bias_ref[...]
    tm = sel.shape[1]
    ninf = -jnp.inf

    gi = lax.broadcasted_iota(jnp.int32, (GROUP_SIZE, tm), 0).astype(F32)
    g_rows = []
    for g in range(N_GROUPS):
        seg = sel[g * GROUP_SIZE:(g + 1) * GROUP_SIZE, :]
        m1 = jnp.max(seg, axis=0, keepdims=True)
        first = _first_index(seg == m1, gi, float(GROUP_SIZE))
        m2 = jnp.max(jnp.where(gi == first, ninf, seg), axis=0, keepdims=True)
        g_rows.append(m1 + m2)
    g_score = jnp.concatenate(g_rows, axis=0)

    ggi = lax.broadcasted_iota(jnp.int32, (N_GROUPS, tm), 0).astype(F32)
    g_keep = jnp.zeros((N_GROUPS, tm), F32)
    cur = g_score
    for _ in range(TOPK_GROUPS):
        mx = jnp.max(cur, axis=0, keepdims=True)
        hit = ggi == _first_index(cur == mx, ggi, float(N_GROUPS))
        g_keep = jnp.where(hit, 1.0, g_keep)
        cur = jnp.where(hit, ninf, cur)

    keep = jnp.concatenate(
        [jnp.broadcast_to(g_keep[g:g + 1, :], (GROUP_SIZE, tm)) for g in range(N_GROUPS)], axis=0)
    cur = jnp.where(keep > 0.0, sel, NEG)
    ei = lax.broadcasted_iota(jnp.int32, (N_EXPERTS, tm), 0).astype(F32)
    idx_rows, w_rows = [], []
    chosen = jnp.zeros((N_EXPERTS, tm), F32)
    for _ in range(TOP_K):
        mx = jnp.max(cur, axis=0, keepdims=True)
        first = _first_index(cur == mx, ei, float(N_EXPERTS))
        hit = ei == first
        idx_rows.append(first)
        w_rows.append(jnp.sum(jnp.where(hit, scores, 0.0), axis=0, keepdims=True))
        cur = jnp.where(hit, ninf, cur)
        chosen = jnp.where(hit, 1.0, chosen)
    w = jnp.concatenate(w_rows, axis=0)
    idx_ref[...] = jnp.concatenate(idx_rows, axis=0).astype(jnp.int32)
    wts_ref[...] = w / jnp.sum(w, axis=0, keepdims=True) * ROUTED_SCALE

    t_src = lax.broadcasted_iota(jnp.int32, (tm, tm), 0)
    t_dst = lax.broadcasted_iota(jnp.int32, (tm, tm), 1)
    before = (t_src < t_dst).astype(BF16)
    rank = _dot(chosen.astype(BF16), before) + run_ref[...]
    pos_ref[...] = jnp.concatenate(
        [jnp.sum(jnp.where(ei == r, rank, 0.0), axis=0, keepdims=True) for r in idx_rows],
        axis=0).astype(jnp.int32)
    run = run_ref[...] + jnp.sum(chosen, axis=1, keepdims=True)
    run_ref[...] = run
    cnt_ref[...] = run.astype(jnp.int32)


def _router_call(x, w_router, router_bias):
    t, d = x.shape
    tm = min(512, t)
    wt = w_router.T
    wh = wt.astype(BF16)
    wl = (wt - wh.astype(F32)).astype(BF16)
    kt = lambda: pl.BlockSpec((TOP_K, tm), lambda i: (0, i))
    return pl.pallas_call(
        _router_kernel,
        grid=(t // tm,),
        in_specs=[pl.BlockSpec((tm, d), lambda i: (i, 0)),
                  pl.BlockSpec((N_EXPERTS, d), lambda i: (0, 0)),
                  pl.BlockSpec((N_EXPERTS, d), lambda i: (0, 0)),
                  pl.BlockSpec((N_EXPERTS, 1), lambda i: (0, 0))],
        out_specs=[kt(), kt(), kt(), pl.BlockSpec((N_EXPERTS, 1), lambda i: (0, 0))],
        out_shape=[jax.ShapeDtypeStruct((TOP_K, t), jnp.int32),
                   jax.ShapeDtypeStruct((TOP_K, t), F32),
                   jax.ShapeDtypeStruct((TOP_K, t), jnp.int32),
                   jax.ShapeDtypeStruct((N_EXPERTS, 1), jnp.int32)],
        scratch_shapes=[pltpu.VMEM((N_EXPERTS, 1), F32)],
        compiler_params=_cparams("arbitrary"),
        name="router",
    )(x, wh, wl, router_bias.astype(F32).reshape(N_EXPERTS, 1))


def _dest_kernel(idx_ref, pos_ref, start_ref, dest_ref):
    tm = idx_ref.shape[1]
    ei = lax.broadcasted_iota(jnp.int32, (N_EXPERTS, tm), 0)
    start = start_ref[...].astype(F32)
    rows = [jnp.sum(jnp.where(ei == idx_ref[k:k + 1, :], start, 0.0), axis=0, keepdims=True)
            for k in range(TOP_K)]
    dest_ref[...] = pos_ref[...] + jnp.concatenate(rows, axis=0).astype(jnp.int32)


def _dest_call(idx_kt, pos_kt, p_start):
    t = idx_kt.shape[1]
    tm = min(1024, t)
    kt = lambda: pl.BlockSpec((TOP_K, tm), lambda i: (0, i))
    return pl.pallas_call(
        _dest_kernel,
        grid=(t // tm,),
        in_specs=[kt(), kt(), pl.BlockSpec((N_EXPERTS, 1), lambda i: (0, 0))],
        out_specs=kt(),
        out_shape=jax.ShapeDtypeStruct((TOP_K, t), jnp.int32),
        compiler_params=_cparams("parallel"),
        name="moe_dest",
    )(idx_kt, pos_kt, p_start.reshape(N_EXPERTS, 1))


LANES = 128
ROW_SUB = D_MODEL // (2 * LANES)
U32 = jnp.uint32
HIGH_HALF = 0xFFFF0000
DISPATCH_TILE = 256


def _row_tile(ref, r):
    return ref.at[pl.ds(pl.multiple_of(r * ROW_SUB, ROW_SUB), ROW_SUB), :]


def _lane_chunk(first_row, n_rows, j):
    return pl.ds(first_row * ROW_SUB + j, n_rows, stride=ROW_SUB)


def _bf16_bits(v):
    return lax.bitcast_convert_type(v.astype(BF16).astype(F32), U32)


def _to_row_tiles(dst_ref, v):
    for j in range(ROW_SUB):
        lo = _bf16_bits(v[:, (2 * j) * LANES:(2 * j + 1) * LANES]) >> 16
        hi = _bf16_bits(v[:, (2 * j + 1) * LANES:(2 * j + 2) * LANES])
        dst_ref[_lane_chunk(0, v.shape[0], j), :] = hi | lo


def _from_row_tiles(src_ref, first_row, n_rows):
    chunks = []
    for j in range(ROW_SUB):
        w = src_ref[_lane_chunk(first_row, n_rows, j), :]
        chunks.append(lax.bitcast_convert_type(w << 16, F32))
        chunks.append(lax.bitcast_convert_type(w & U32(HIGH_HALF), F32))
    return jnp.concatenate(chunks, axis=1)


def _dispatch_kernel(valid_ref, dest_ref, x_ref, p_ref, ws13_ref, ws2_ref, wple_ref, wpg_ref,
                     xs_hbm, dense_ref, xt, zbuf, sem, zsem):
    i = pl.program_id(0)
    tm = xt.shape[1] // ROW_SUB
    bm = zbuf.shape[0] // ROW_SUB
    n_blk = valid_ref.shape[0]
    slot = i % 2

    @pl.when(pl.program_id(0) == 0)
    def _():
        zbuf[...] = jnp.zeros_like(zbuf)

        def zero_copy(i):
            first = pl.multiple_of(i * (bm * ROW_SUB), bm * ROW_SUB)
            return pltpu.make_async_copy(zbuf, xs_hbm.at[pl.ds(first, bm * ROW_SUB), :], zsem)

        def start(i, carry):
            @pl.when(valid_ref[i] < bm)
            def _():
                zero_copy(i).start()
            return carry

        def wait(i, carry):
            @pl.when(valid_ref[i] < bm)
            def _():
                zero_copy(i).wait()
            return carry

        lax.fori_loop(0, n_blk, start, 0)
        lax.fori_loop(0, n_blk, wait, 0)

    stage = xt.at[slot]
    _to_row_tiles(stage, x_ref[...].astype(F32))

    def body(r, carry):
        for k in range(TOP_K):
            pltpu.make_async_copy(_row_tile(stage, r), _row_tile(xs_hbm, dest_ref[k, r]),
                                  sem.at[slot]).start(priority=k % 2)
        return carry

    lax.fori_loop(0, tm, body, 0)
    dense_ref[...] = _shared_ple(x_ref[...], p_ref[...], ws13_ref, ws2_ref, wple_ref, wpg_ref)

    def drain(s):
        for k in range(TOP_K):
            pltpu.make_async_copy(xt.at[s], xs_hbm.at[pl.ds(0, tm * ROW_SUB), :], sem.at[s]).wait()

    @pl.when(i > 0)
    def _():
        drain(1 - slot)

    @pl.when(i == pl.num_programs(0) - 1)
    def _():
        drain(slot)


def _dispatch_call(x, dest_kt, blk_valid, p, ws13, ws2, wple, wpg):
    t, d = x.shape
    tm = min(DISPATCH_TILE, t)
    bm = MOE_BLOCK
    n_blk = blk_valid.shape[0]
    full = lambda a: pl.BlockSpec(a.shape, lambda i, v: (0, 0))
    grid_spec = pltpu.PrefetchScalarGridSpec(
        num_scalar_prefetch=1,
        grid=(t // tm,),
        in_specs=[pl.BlockSpec((TOP_K, tm), lambda i, v: (0, i), memory_space=pltpu.SMEM),
                  pl.BlockSpec((tm, d), lambda i, v: (i, 0)),
                  pl.BlockSpec((tm, PLE_DIM), lambda i, v: (i, 0)),
                  full(ws13), full(ws2), full(wple), full(wpg)],
        out_specs=[pl.BlockSpec(memory_space=pl.ANY), pl.BlockSpec((tm, d), lambda i, v: (i, 0))],
        scratch_shapes=[pltpu.VMEM((2, tm * ROW_SUB, LANES), U32), pltpu.VMEM((bm * ROW_SUB, LANES), U32),
                        pltpu.SemaphoreType.DMA((2,)), pltpu.SemaphoreType.DMA(())],
    )
    return pl.pallas_call(
        _dispatch_kernel,
        grid_spec=grid_spec,
        out_shape=[jax.ShapeDtypeStruct((n_blk * bm * ROW_SUB, LANES), U32),
                   jax.ShapeDtypeStruct((t, d), F32)],
        compiler_params=_cparams("arbitrary"),
        name="moe_dispatch",
    )(blk_valid, dest_kt, x, p, ws13, ws2, wple, wpg)


def _expert_kernel(blk_e_ref, nused_ref, xs_ref, w13_ref, w2_ref, o_ref):
    i = pl.program_id(0)

    @pl.when(i < nused_ref[0])
    def _():
        xb = _from_row_tiles(xs_ref, 0, xs_ref.shape[0] // ROW_SUB).astype(BF16)
        h13 = _dot(xb, w13_ref[0])
        hid = _silu(h13[:, :D_EXPERT]) * h13[:, D_EXPERT:]
        _to_row_tiles(o_ref, _dot(hid.astype(BF16), w2_ref[0]))

    @pl.when(i >= nused_ref[0])
    def _():
        o_ref[...] = jnp.zeros_like(o_ref)


def _expert_call(xs, blk_e, n_used, w13, w2):
    bm = MOE_BLOCK
    n_blk = blk_e.shape[0]
    d = D_MODEL
    xs_idx = lambda i, e, n: (jnp.minimum(i, jnp.maximum(n[0] - 1, 0)), 0)
    grid_spec = pltpu.PrefetchScalarGridSpec(
        num_scalar_prefetch=2,
        grid=(n_blk,),
        in_specs=[pl.BlockSpec((bm * ROW_SUB, LANES), xs_idx),
                  pl.BlockSpec((1, d, 2 * D_EXPERT), lambda i, e, n: (e[i], 0, 0)),
                  pl.BlockSpec((1, D_EXPERT, d), lambda i, e, n: (e[i], 0, 0))],
        out_specs=pl.BlockSpec((bm * ROW_SUB, LANES), lambda i, e, n: (i, 0)),
    )
    return pl.pallas_call(
        _expert_kernel,
        grid_spec=grid_spec,
        out_shape=jax.ShapeDtypeStruct((n_blk * bm * ROW_SUB, LANES), U32),
        compiler_params=_cparams("arbitrary"),
        name="routed_experts",
    )(blk_e, n_used, xs, w13, w2)


def _combine_kernel(dest_ref, destn_ref, ys_hbm, w_ref, dense_ref, x_ref, g_ref, b_ref,
                    o_ref, ob_ref, gbuf, sem, *, alpha):
    i = pl.program_id(0)
    n = pl.num_programs(0)
    rows = gbuf.shape[1] // ROW_SUB
    tm = rows // TOP_K
    cur = i % 2

    def issue(d_ref, s):
        def body(r, carry):
            for k in range(TOP_K):
                pltpu.make_async_copy(_row_tile(ys_hbm, d_ref[k, r]), _row_tile(gbuf.at[s], k * tm + r),
                                      sem.at[s]).start(priority=k % 2)
            return carry
        lax.fori_loop(0, tm, body, 0)

    @pl.when(i == 0)
    def _():
        issue(dest_ref, 0)

    @pl.when(i + 1 < n)
    def _():
        issue(destn_ref, 1 - cur)

    pltpu.make_async_copy(ys_hbm.at[pl.ds(0, rows * ROW_SUB), :], gbuf.at[cur], sem.at[cur]).wait()
    w = w_ref[...]
    routed = _from_row_tiles(gbuf.at[cur], 0, tm) * w[:, 0:1]
    for k in range(1, TOP_K):
        routed = routed + _from_row_tiles(gbuf.at[cur], k * tm, tm) * w[:, k:k + 1]
    y = _layer_norm_rows(alpha * x_ref[...] + routed + dense_ref[...], g_ref[...], b_ref[...])
    o_ref[...] = y
    ob_ref[...] = y.astype(BF16)


def _combine_call(ys, dest_kt, wts, dense, x, g, b, alpha):
    t, d = x.shape
    tm = min(COMBINE_TILE, t)
    nt = t // tm
    rows = tm * TOP_K
    tile = lambda: pl.BlockSpec((tm, d), lambda i: (i, 0))
    vec = lambda: pl.BlockSpec((1, d), lambda i: (0, 0))
    return pl.pallas_call(
        functools.partial(_combine_kernel, alpha=alpha),
        grid=(nt,),
        in_specs=[pl.BlockSpec((TOP_K, tm), lambda i: (0, i), memory_space=pltpu.SMEM),
                  pl.BlockSpec((TOP_K, tm), lambda i: (0, jnp.minimum(i + 1, nt - 1)),
                               memory_space=pltpu.SMEM),
                  pl.BlockSpec(memory_space=pl.ANY),
                  pl.BlockSpec((tm, TOP_K), lambda i: (i, 0)),
                  tile(), tile(), vec(), vec()],
        out_specs=[tile(), tile()],
        out_shape=[jax.ShapeDtypeStruct((t, d), F32), jax.ShapeDtypeStruct((t, d), BF16)],
        scratch_shapes=[pltpu.VMEM((2, rows * ROW_SUB, LANES), U32), pltpu.SemaphoreType.DMA((2,))],
        compiler_params=_cparams("arbitrary"),
        name="combine_ln2",
    )(dest_kt, dest_kt, ys, wts, dense, x, g.reshape(1, d), b.reshape(1, d))


def _block_tables(counts, n_blk):
    bm = MOE_BLOCK
    padded = (counts + bm - 1) // bm * bm
    p_end = jnp.cumsum(padded)
    p_start = p_end - padded
    blk_row = jnp.arange(n_blk, dtype=jnp.int32) * bm
    blk_e = jnp.minimum(jnp.sum(p_end[None, :] <= blk_row[:, None], axis=1), N_EXPERTS - 1).astype(jnp.int32)
    blk_valid = jnp.clip(counts[blk_e] - (blk_row - p_start[blk_e]), 0, bm).astype(jnp.int32)
    n_used = (p_end[-1] // bm).astype(jnp.int32).reshape(1)
    return p_start.astype(jnp.int32), blk_e, blk_valid, n_used


def _prep_layer(li, lower, w_in, b_in, conv_w, conv_b, norm_a, norm_b, w_out_a, w_out_b, w_o,
                ln1_g, ln1_b, w_ple, w_ple_gate, w_router, router_bias, w1, w3, w2, ws1, ws3, ws2,
                ln2_g, ln2_b):
    d = D_MODEL
    w = w_in[li]
    bi = b_in[li]
    seg = lambda a, lo, hi: a[..., lo:hi]
    o_qkv, o_oa, o_g = 0, 3 * d, 4 * d
    o_qb = 4 * d + 4 * H_A
    o_ff, o_ib, o_gb, o_ga = o_qb + d, o_qb + 3 * d, o_qb + 4 * d, o_qb + 5 * d
    cat = lambda parts: jnp.concatenate(parts, axis=-1)
    raw_cols = [(o_qkv, o_qkv + 3 * d), (o_ib, o_ib + d)]
    sig_cols = [(o_oa, o_oa + d), (o_ga, o_ga + 2 * d)]
    silu_cols = [(o_qb, o_qb + d), (o_gb, o_gb + d)]
    return dict(
        w_raw=cat([seg(w, *c) for c in raw_cols]).astype(BF16), b_raw=cat([seg(bi, *c) for c in raw_cols]),
        w_sig=cat([seg(w, *c) for c in sig_cols]).astype(BF16), b_sig=cat([seg(bi, *c) for c in sig_cols]),
        w_silu=cat([seg(w, *c) for c in silu_cols]).astype(BF16), b_silu=cat([seg(bi, *c) for c in silu_cols]),
        w_fg=seg(w, o_ff, o_ff + 2 * d).astype(BF16), b_fg=seg(bi, o_ff, o_ff + 2 * d),
        w_g=seg(w, o_g, o_g + 4 * H_A).astype(BF16), b_g=seg(bi, o_g, o_g + 4 * H_A),
        lower=lower[li].reshape(-1),
        conv_w=conv_w[li], conv_b=conv_b[li], norm_a=norm_a[li], norm_b=norm_b[li],
        w_out_a=w_out_a[li].astype(BF16), w_out_b=w_out_b[li].astype(BF16), w_o=w_o[li].astype(BF16),
        ln1_g=ln1_g[li], ln1_b=ln1_b[li],
        w_ple=w_ple[li].astype(BF16), w_ple_gate=w_ple_gate[li].astype(BF16),
        w_router=w_router[li], router_bias=router_bias[li],
        w13=jnp.concatenate([w1[li], w3[li]], axis=-1).astype(BF16), w2=w2[li].astype(BF16),
        ws13=jnp.concatenate([ws1[li], ws3[li]], axis=-1).astype(BF16), ws2=ws2[li].astype(BF16),
        ln2_g=ln2_g[li], ln2_b=ln2_b[li],
    )


def _layer(x, xb, p, lp, bsz, s, alpha):
    t, d = x.shape
    a = _proj_act(xb, lp["w_raw"], lp["b_raw"], "raw", "proj_raw")
    sg = _proj_act(xb, lp["w_sig"], lp["b_sig"], "sigmoid", "proj_sigmoid")
    sl = _proj_act(xb, lp["w_silu"], lp["b_silu"], "silu", "proj_silu")
    kf, lf = _proj_forget(xb, lp["w_fg"], lp["b_fg"], lp["lower"])
    gc, gr = _proj_gates(xb, lp["w_g"], lp["b_g"])

    a3 = a.reshape(bsz, s, 4 * d)
    qk3 = _conv_silu(a3, lp["conv_w"], lp["conv_b"])
    gc3 = gc.reshape(bsz, s, 4 * H_A)
    haf = _mlstm_dir(qk3, a3, gc3, gr, False).reshape(t, d)
    hab = _mlstm_dir(qk3, a3, gc3, gr, True).reshape(t, d)

    sl3 = sl.reshape(bsz, s, 2 * d)
    kf3 = kf.reshape(bsz, s, 2 * d)
    lf3 = lf.reshape(bsz, s, 2 * d)
    hbf = _hgrn_dir(sl3, kf3, a3, lf3, False).reshape(t, d)
    hbb = _hgrn_dir(sl3, kf3, a3, lf3, True).reshape(t, d)

    x1, x1b = _mix_call(x, haf, hab, hbf, hbb, sg, sl, lp["norm_a"], lp["norm_b"],
                        lp["w_out_a"], lp["w_out_b"], lp["w_o"], lp["ln1_g"], lp["ln1_b"], alpha)

    idx_kt, wts_kt, pos_kt, counts = _router_call(x1, lp["w_router"], lp["router_bias"])
    n_blk = t * TOP_K // MOE_BLOCK + N_EXPERTS
    p_start, blk_e, blk_valid, n_used = _block_tables(counts.reshape(-1), n_blk)
    dest_kt = _dest_call(idx_kt, pos_kt, p_start)
    xs, dense = _dispatch_call(x1b, dest_kt, blk_valid, p, lp["ws13"], lp["ws2"], lp["w_ple"], lp["w_ple_gate"])
    ys = _expert_call(xs, blk_e, n_used, lp["w13"], lp["w2"])
    return _combine_call(ys, dest_kt, wts_kt.T, dense, x1, lp["ln2_g"], lp["ln2_b"], alpha)


def kernel(x_prompt, x_sample, p_prompt, p_sample, emb_ln_g, emb_ln_b, w_in, b_in, conv_w, conv_b, norm_a, norm_b, lb_param, w_out_a, w_out_b, w_o, ln1_g, ln1_b, w_ple, w_ple_gate, w_router, router_bias, w1, w3, w2, ws1, ws3, ws2, ln2_g, ln2_b):
    depth = w_in.shape[0]
    alpha = (2.0 * depth) ** 0.25
    lb_soft = jax.nn.softmax(lb_param.astype(F32), axis=0)
    lower = jnp.cumsum(lb_soft, axis=0) - lb_soft[0:1]
    layers = [_prep_layer(li, lower, w_in, b_in, conv_w, conv_b, norm_a, norm_b, w_out_a, w_out_b, w_o,
                          ln1_g, ln1_b, w_ple, w_ple_gate, w_router, router_bias, w1, w3, w2,
                          ws1, ws3, ws2, ln2_g, ln2_b) for li in range(depth)]

    def trunk(x3, p4):
        bsz, s, d = x3.shape
        x, xb = _layer_norm_call(x3.reshape(bsz * s, d), emb_ln_g, emb_ln_b)
        for li in range(depth):
            x, xb = _layer(x, xb, p4[li].reshape(bsz * s, -1), layers[li], bsz, s, alpha)
        return x.reshape(bsz, s, d)

    return (trunk(x_prompt, p_prompt), trunk(x_sample, p_sample))
```
